```python
import math
import jax, jax.numpy as jnp
from jax import lax
import numpy as np

D_MODEL = 4096
BATCH = 2
SEQ = 8192
DEPTH = 2

D_MIX = D_MODEL
HEAD_DIM = 128
ATTN_GROUPS = ((128, 1), (512, 4), (2048, 16))
ATTN_HEADS_PER_GROUP = 4
ATTN_HEADS = ATTN_HEADS_PER_GROUP * len(ATTN_GROUPS)
D_ATTN = ATTN_HEADS * HEAD_DIM
ATTN_BLOCK = 128
ROPE_THETA = 500000.0
ROPE_DIM = HEAD_DIM // 4
D_SSM = 1536
SSM_HEAD_DIM = 64
SSM_HEADS = D_SSM // SSM_HEAD_DIM
SSM_GROUPS = 4
SSM_STATE = 128
SSM_CONV = 4
SSM_CHUNK = 128
D_XBC = D_SSM + 2 * SSM_GROUPS * SSM_STATE
D_SC = D_MIX - D_ATTN - D_SSM
SC_CONV = 3
D_IN = 3 * D_ATTN + D_SSM + D_XBC + SSM_HEADS + 3 * D_SC
D_FF = 4 * D_MODEL
D_PLE = 256
DN_ALPHA = (2.0 * DEPTH) ** 0.25
DN_BETA = (8.0 * DEPTH) ** -0.25
LN_EPS = 1e-5
RMS_EPS = 1e-5

kernel_name = 'hymba_ssd_shortconv_dilated_attn_deepnorm'


def layer_norm(x, g, b):
    xf = x.astype(jnp.float32)
    mu = jnp.mean(xf, axis=-1, keepdims=True)
    var = jnp.mean(jnp.square(xf - mu), axis=-1, keepdims=True)
    return ((xf - mu) * lax.rsqrt(var + LN_EPS) * g.astype(jnp.float32) + b.astype(jnp.float32)).astype(x.dtype)


def causal_dwconv(x, w, b=None):
    k_len, ch = w.shape
    y = lax.conv_general_dilated(x, w[:, None, :].astype(x.dtype), window_strides=(1,),
                                 padding=[(k_len - 1, 0)],
                                 dimension_numbers=('NWC', 'WIO', 'NWC'),
                                 feature_group_count=ch)
    if b is not None:
        y = y + b.astype(x.dtype)
    return y


def split_projection(u):
    sizes = (D_ATTN, D_ATTN, D_ATTN, D_SSM, D_XBC, SSM_HEADS, D_SC, D_SC, D_SC)
    offs = []
    acc = 0
    for s in sizes[:-1]:
        acc += s
        offs.append(acc)
    return jnp.split(u, offs, axis=-1)


def partial_rope(x, pos):
    half = ROPE_DIM // 2
    inv_freq = ROPE_THETA ** (-jnp.arange(half, dtype=jnp.float32) / half)
    ang = pos.astype(jnp.float32)[:, None] * inv_freq[None, :]
    cos = jnp.cos(ang)[None, :, None, :]
    sin = jnp.sin(ang)[None, :, None, :]
    x1 = x[..., :half].astype(jnp.float32)
    x2 = x[..., half:ROPE_DIM].astype(jnp.float32)
    rot = jnp.concatenate([x1 * cos - x2 * sin, x2 * cos + x1 * sin], axis=-1).astype(x.dtype)
    return jnp.concatenate([rot, x[..., ROPE_DIM:]], axis=-1)


def dilated_window_attention(q, k, v, window, dilation):
    bsz, s_len, n_h, d_h = q.shape
    n_back = window // dilation
    blk = ATTN_BLOCK
    assert n_back <= blk
    span = dilation * blk
    s_pad = -(-s_len // span) * span
    sub_len = s_pad // dilation
    n_blk = sub_len // blk

    def to_blocks(t):
        t = jnp.pad(t, ((0, 0), (0, s_pad - s_len), (0, 0), (0, 0)))
        t = t.reshape(bsz, sub_len, dilation, n_h, d_h).swapaxes(1, 2)
        return t.reshape(bsz, dilation, n_blk, blk, n_h, d_h)

    def with_prev(t):
        prev = jnp.pad(t, ((0, 0), (0, 0), (1, 0), (0, 0), (0, 0), (0, 0)))[:, :, :-1]
        return jnp.concatenate([prev, t], axis=3)

    qb = to_blocks(q)
    kb = with_prev(to_blocks(k))
    vb = with_prev(to_blocks(v))
    scores = jnp.einsum('brnqhd,brnkhd->brnhqk', qb, kb,
                        preferred_element_type=jnp.float32) * (d_h ** -0.5)
    qi = jnp.arange(blk)[:, None]
    ki = jnp.arange(2 * blk)[None, :]
    dist = blk + qi - ki
    first = jnp.arange(n_blk)[:, None, None] == 0
    valid = (dist >= 0) & (dist <= n_back) & ~(first & (ki < blk))
    scores = jnp.where(valid[None, None, :, None], scores, -jnp.inf)
    m = jnp.max(scores, axis=-1, keepdims=True)
    w = jnp.exp(scores - m)
    den = jnp.sum(w, axis=-1, keepdims=True)
    o = jnp.einsum('brnhqk,brnkhd->brnqhd', w, vb.astype(jnp.float32)) / jnp.swapaxes(den, 3, 4)
    lse = jnp.swapaxes((m + jnp.log(den))[..., 0], 3, 4)

    def from_blocks(t):
        rest = t.shape[4:]
        t = t.reshape(bsz, dilation, sub_len, *rest).swapaxes(1, 2)
        return t.reshape(bsz, s_pad, *rest)[:, :s_len]

    return from_blocks(o).astype(q.dtype), from_blocks(lse)


def dilated_attention_mixer(q, k, v, pos):
    bsz, s_len, _ = q.shape
    q = partial_rope(q.reshape(bsz, s_len, ATTN_HEADS, HEAD_DIM), pos)
    k = partial_rope(k.reshape(bsz, s_len, ATTN_HEADS, HEAD_DIM), pos)
    v = v.reshape(bsz, s_len, ATTN_HEADS, HEAD_DIM)
    outs, lses = [], []
    for g, (window, dil) in enumerate(ATTN_GROUPS):
        hs = slice(g * ATTN_HEADS_PER_GROUP, (g + 1) * ATTN_HEADS_PER_GROUP)
        o, l = dilated_window_attention(q[:, :, hs], k[:, :, hs], v[:, :, hs], window, dil)
        outs.append(o)
        lses.append(l)
    mix_w = jax.nn.softmax(jnp.stack(lses, axis=2), axis=2)
    o = jnp.stack(outs, axis=2).astype(jnp.float32) * mix_w[..., None]
    return o.reshape(bsz, s_len, D_ATTN).astype(q.dtype)


def ssd_chunked(x, dt, a, b_in, c_in):
    bsz, s_len, n_h, hd = x.shape
    n_g, n_s = b_in.shape[2], b_in.shape[3]
    hpg = n_h // n_g
    q_len = SSM_CHUNK
    n_c = s_len // q_len
    xdt = (x.astype(jnp.float32) * dt[..., None]).reshape(bsz, n_c, q_len, n_g, hpg, hd)
    adt = (dt * a).reshape(bsz, n_c, q_len, n_g, hpg)
    bc = b_in.astype(jnp.float32).reshape(bsz, n_c, q_len, n_g, n_s)
    cc = c_in.astype(jnp.float32).reshape(bsz, n_c, q_len, n_g, n_s)
    a_cs = jnp.cumsum(adt, axis=2)
    seg = a_cs[:, :, :, None] - a_cs[:, :, None, :]
    causal = jnp.tril(jnp.ones((q_len, q_len), dtype=bool))[:, :, None, None]
    decay = jnp.exp(jnp.where(causal, seg, -jnp.inf))
    cb = jnp.einsum('bclgn,bcsgn->bclsg', cc, bc)
    y_diag = jnp.einsum('bclsg,bclsgj,bcsgjp->bclgjp', cb, decay, xdt)
    decay_to_end = jnp.exp(a_cs[:, :, -1:] - a_cs)
    states = jnp.einsum('bcsgn,bcsgj,bcsgjp->bcgjpn', bc, decay_to_end, xdt)
    chunk_decay = jnp.exp(a_cs[:, :, -1])

    def step(h, inp):
        st, dec = inp
        return h * dec[..., None, None] + st, h

    h0 = jnp.zeros((bsz, n_g, hpg, hd, n_s), jnp.float32)
    _, prev = lax.scan(step, h0, (states.swapaxes(0, 1), chunk_decay.swapaxes(0, 1)))
    prev = prev.swapaxes(0, 1)
    y_off = jnp.einsum('bclgn,bcgjpn,bclgj->bclgjp', cc, prev, jnp.exp(a_cs))
    return (y_diag + y_off).reshape(bsz, s_len, n_h, hd)


def ssd_mixer(z, xbc, dt_raw, conv_w, conv_b, dt_bias, a_log, d_skip, norm_w):
    bsz, s_len, _ = z.shape
    xbc = jax.nn.silu(causal_dwconv(xbc, conv_w, conv_b))
    xs, bs, cs = jnp.split(xbc, [D_SSM, D_SSM + SSM_GROUPS * SSM_STATE], axis=-1)
    xs = xs.reshape(bsz, s_len, SSM_HEADS, SSM_HEAD_DIM)
    bs = bs.reshape(bsz, s_len, SSM_GROUPS, SSM_STATE)
    cs = cs.reshape(bsz, s_len, SSM_GROUPS, SSM_STATE)
    dt = jax.nn.softplus(dt_raw.astype(jnp.float32) + dt_bias.astype(jnp.float32))
    a = -jnp.exp(a_log.astype(jnp.float32))
    y = ssd_chunked(xs, dt, a, bs, cs) + d_skip.astype(jnp.float32)[:, None] * xs.astype(jnp.float32)
    y = y.reshape(bsz, s_len, D_SSM) * jax.nn.silu(z.astype(jnp.float32))
    yg = y.reshape(bsz, s_len, SSM_GROUPS, D_SSM // SSM_GROUPS)
    yg = yg * lax.rsqrt(jnp.mean(jnp.square(yg), axis=-1, keepdims=True) + RMS_EPS)
    return (yg.reshape(bsz, s_len, D_SSM) * norm_w.astype(jnp.float32)).astype(z.dtype)


def short_conv_mixer(b_gate, c_gate, h, conv_w):
    return b_gate * causal_dwconv(c_gate * h, conv_w)


def setup_inputs(seed: int = 0) -> dict:
    key = jax.random.key(seed)
    ks = jax.random.split(key, 24)
    f32 = jnp.float32
    nrm = lambda k, shape, scale: jax.random.normal(k, shape, f32) * scale
    x = nrm(ks[0], (BATCH, SEQ, D_MODEL), 1.0)
    p = nrm(ks[1], (DEPTH, BATCH, SEQ, D_PLE), 1.0)
    w_in = nrm(ks[2], (DEPTH, D_MODEL, D_IN), D_MODEL ** -0.5)
    ssm_conv_w = nrm(ks[3], (DEPTH, SSM_CONV, D_XBC), SSM_CONV ** -0.5)
    ssm_conv_b = nrm(ks[4], (DEPTH, D_XBC), 0.02)
    dt0 = jnp.exp(jax.random.uniform(ks[5], (DEPTH, SSM_HEADS), f32,
                                     minval=math.log(1e-3), maxval=math.log(1e-1)))
    ssm_dt_bias = dt0 + jnp.log(-jnp.expm1(-dt0))
    ssm_a_log = jnp.log(jax.random.uniform(ks[6], (DEPTH, SSM_HEADS), f32, minval=1.0, maxval=16.0))
    ssm_d = 1.0 + nrm(ks[7], (DEPTH, SSM_HEADS), 0.1)
    ssm_norm_w = 1.0 + nrm(ks[8], (DEPTH, D_SSM), 0.1)
    sc_conv_w = nrm(ks[9], (DEPTH, SC_CONV, D_SC), SC_CONV ** -0.5)
    w_out = nrm(ks[10], (DEPTH, D_MIX, D_MODEL), DN_BETA * D_MIX ** -0.5)
    ln1_g = 1.0 + nrm(ks[11], (DEPTH, D_MODEL), 0.05)
    ln1_b = nrm(ks[12], (DEPTH, D_MODEL), 0.02)
    w_up = nrm(ks[13], (DEPTH, D_MODEL, D_FF), D_MODEL ** -0.5)
    w_down = nrm(ks[14], (DEPTH, D_FF, D_MODEL), DN_BETA * D_FF ** -0.5)
    ln2_g = 1.0 + nrm(ks[15], (DEPTH, D_MODEL), 0.05)
    ln2_b = nrm(ks[16], (DEPTH, D_MODEL), 0.02)
    w_pe = nrm(ks[17], (DEPTH, D_PLE, D_MODEL), DN_BETA * D_PLE ** -0.5)
    w_gate = nrm(ks[18], (DEPTH, D_MODEL, D_MODEL), D_MODEL ** -0.5)
    ln3_g = 1.0 + nrm(ks[19], (DEPTH, D_MODEL), 0.05)
    ln3_b = nrm(ks[20], (DEPTH, D_MODEL), 0.02)
    return {'x': x, 'p': p, 'w_in': w_in, 'ssm_conv_w': ssm_conv_w, 'ssm_conv_b': ssm_conv_b,
            'ssm_dt_bias': ssm_dt_bias, 'ssm_a_log': ssm_a_log, 'ssm_d': ssm_d,
            'ssm_norm_w': ssm_norm_w, 'sc_conv_w': sc_conv_w, 'w_out': w_out,
            'ln1_g': ln1_g, 'ln1_b': ln1_b, 'w_up': w_up, 'w_down': w_down,
            'ln2_g': ln2_g, 'ln2_b': ln2_b, 'w_pe': w_pe, 'w_gate': w_gate,
            'ln3_g': ln3_g, 'ln3_b': ln3_b}


def reference(x, p, w_in, ssm_conv_w, ssm_conv_b, ssm_dt_bias, ssm_a_log, ssm_d, ssm_norm_w,
              sc_conv_w, w_out, ln1_g, ln1_b, w_up, w_down, ln2_g, ln2_b, w_pe, w_gate,
              ln3_g, ln3_b):
    s_len = x.shape[1]
    pos = jnp.arange(s_len, dtype=jnp.int32)
    for i in range(DEPTH):
        u = jnp.einsum('bsd,de->bse', x, w_in[i])
        q, k, v, z, xbc, dt_raw, sc_b, sc_c, sc_h = split_projection(u)
        y_attn = dilated_attention_mixer(q, k, v, pos)
        y_ssm = ssd_mixer(z, xbc, dt_raw, ssm_conv_w[i], ssm_conv_b[i], ssm_dt_bias[i],
                          ssm_a_log[i], ssm_d[i], ssm_norm_w[i])
        y_sc = short_conv_mixer(sc_b, sc_c, sc_h, sc_conv_w[i])
        mixed = jnp.einsum('bse,ed->bsd', jnp.concatenate([y_attn, y_ssm, y_sc], axis=-1), w_out[i])
        x = layer_norm(DN_ALPHA * x + mixed, ln1_g[i], ln1_b[i])
        hid = jnp.square(jax.nn.relu(jnp.einsum('bsd,df->bsf', x, w_up[i])))
        x = layer_norm(DN_ALPHA * x + jnp.einsum('bsf,fd->bsd', hid, w_down[i]), ln2_g[i], ln2_b[i])
        gate = jax.nn.sigmoid(jnp.einsum('bsd,de->bse', x, w_gate[i]))
        emb = jnp.einsum('bsk,kd->bsd', p[i], w_pe[i])
        x = layer_norm(DN_ALPHA * x + gate * emb, ln3_g[i], ln3_b[i])
    return x
```

```python
import functools
import math

import jax
import jax.numpy as jnp
from jax import lax
from jax.experimental import pallas as pl
from jax.experimental.pallas import tpu as pltpu

HEAD_DIM = 128
ATTN_GROUPS = ((128, 1), (512, 4), (2048, 16))
ATTN_HPG = 4
ATTN_BLOCK = 128
D_ATTN = ATTN_HPG * len(ATTN_GROUPS) * HEAD_DIM
D_AGRP = ATTN_HPG * HEAD_DIM
ROPE_THETA = 500000.0
ROPE_HALF = HEAD_DIM // 8
D_SSM = 1536
SSM_HEAD_DIM = 64
SSM_HEADS = D_SSM // SSM_HEAD_DIM
SSM_GROUPS = 4
SSM_HPG = SSM_HEADS // SSM_GROUPS
SSM_GW = D_SSM // SSM_GROUPS
SSM_STATE = 128
SSM_CONV = 4
SSM_CHUNK = 128
D_SC = 1024
SC_CONV = 3
LN_EPS = 1e-5
RMS_EPS = 1e-5

LANES = 128
SUBLANES = 8
VMEM_LIMIT_BYTES = 56 * 1024 * 1024

BF16 = jnp.bfloat16
F32 = jnp.float32

R_XS = 0
R_Z = D_SSM
R_B = 2 * D_SSM
R_C = R_B + SSM_GROUPS * SSM_STATE
R_SC = R_C + SSM_GROUPS * SSM_STATE
R_DT = R_SC + 3 * D_SC
R_WIDTH = R_DT + LANES
R_TN = 384

LN_COL_CHUNK = 512
LN_ROW_CHUNK = 16


def _params(sem):
    return pltpu.CompilerParams(dimension_semantics=sem, vmem_limit_bytes=VMEM_LIMIT_BYTES)


def _tile(n, pref):
    t = min(n, pref)
    while n % t:
        t -= 1
    return t


def _qkv_kernel(x_ref, w_ref, c_ref, s1_ref, s2_ref, o_ref, *, n_rope_tiles):
    acc = jnp.dot(x_ref[...], w_ref[...], preferred_element_type=F32)
    j = pl.program_id(1)
    tn = acc.shape[1]

    @pl.when(j < n_rope_tiles)
    def _():
        rot = (acc * c_ref[...]
               + pltpu.roll(acc, tn - ROPE_HALF, 1) * s1_ref[...]
               + pltpu.roll(acc, ROPE_HALF, 1) * s2_ref[...])
        o_ref[...] = rot.astype(o_ref.dtype)

    @pl.when(j >= n_rope_tiles)
    def _():
        o_ref[...] = acc.astype(o_ref.dtype)


def _plain_kernel(x_ref, w_ref, o_ref):
    o_ref[...] = jnp.dot(x_ref[...], w_ref[...], preferred_element_type=F32).astype(o_ref.dtype)


def _relu2_kernel(x_ref, w_ref, o_ref):
    acc = jnp.dot(x_ref[...], w_ref[...], preferred_element_type=F32)
    o_ref[...] = jnp.square(jnp.maximum(acc, 0.0)).astype(o_ref.dtype)


def _mm_kfull(body, x, w, out_dtype, tm, tn, extra_inputs=(), extra_specs=()):
    t, k = x.shape
    n = w.shape[1]
    return pl.pallas_call(
        body,
        grid=(t // tm, n // tn),
        in_specs=[pl.BlockSpec((tm, k), lambda i, j: (i, 0)),
                  pl.BlockSpec((k, tn), lambda i, j: (0, j)),
                  *extra_specs],
        out_specs=pl.BlockSpec((tm, tn), lambda i, j: (i, j)),
        out_shape=jax.ShapeDtypeStruct((t, n), out_dtype),
        compiler_params=_params(("parallel", "arbitrary")),
    )(x, w, *extra_inputs)


def _mm_ln_kernel(*refs, alpha, gated):
    if gated:
        x_ref, w_ref, r_ref, g_ref, b_ref, p_ref, wpe_ref, o_ref, obf_ref = refs
    else:
        x_ref, w_ref, r_ref, g_ref, b_ref, o_ref, obf_ref = refs
    k = pl.program_id(1)
    tm, d = o_ref.shape
    cw = _tile(d, LN_COL_CHUNK)
    col_chunks = [slice(c * cw, (c + 1) * cw) for c in range(d // cw)]

    @pl.when(k == 0)
    def _():
        o_ref[...] = jnp.zeros_like(o_ref)

    x = x_ref[...]
    for cs in col_chunks:
        o_ref[:, cs] += jnp.dot(x, w_ref[:, cs], preferred_element_type=F32)

    @pl.when(k == pl.num_programs(1) - 1)
    def _():
        if gated:
            p16 = p_ref[...].astype(BF16)
        for cs in col_chunks:
            val = o_ref[:, cs]
            if gated:
                emb = jnp.dot(p16, wpe_ref[:, cs], preferred_element_type=F32)
                val = jax.nn.sigmoid(val) * emb
            o_ref[:, cs] = alpha * r_ref[:, cs] + val

        rows = _tile(tm, LN_ROW_CHUNK)

        def norm_rows(ri, carry):
            rs = pl.ds(pl.multiple_of(ri * rows, rows), rows)
            y = o_ref[rs, :]
            mu = jnp.mean(y, axis=-1, keepdims=True)
            yc = y - mu
            var = jnp.mean(jnp.square(yc), axis=-1, keepdims=True)
            out = yc * lax.rsqrt(var + LN_EPS) * g_ref[...] + b_ref[...]
            o_ref[rs, :] = out
            obf_ref[rs, :] = out.astype(obf_ref.dtype)
            return carry

        lax.fori_loop(0, tm // rows, norm_rows, 0)


def _mm_ln(x, w, resid, gain, bias, alpha, p=None, w_pe=None):
    t, kdim = x.shape
    d = w.shape[1]
    tm = _tile(t, 512)
    tk = _tile(kdim, 512)
    gated = p is not None
    in_specs = [pl.BlockSpec((tm, tk), lambda i, k: (i, k)),
                pl.BlockSpec((tk, d), lambda i, k: (k, 0)),
                pl.BlockSpec((tm, d), lambda i, k: (i, 0)),
                pl.BlockSpec((1, d), lambda i, k: (0, 0)),
                pl.BlockSpec((1, d), lambda i, k: (0, 0))]
    args = [x, w, resid, gain.reshape(1, d), bias.reshape(1, d)]
    if gated:
        dp = p.shape[1]
        in_specs += [pl.BlockSpec((tm, dp), lambda i, k: (i, 0)),
                     pl.BlockSpec((dp, d), lambda i, k: (0, 0), pipeline_mode=pl.Buffered(1))]
        args += [p, w_pe]
    return pl.pallas_call(
        functools.partial(_mm_ln_kernel, alpha=alpha, gated=gated),
        grid=(t // tm, kdim // tk),
        in_specs=in_specs,
        out_specs=[pl.BlockSpec((tm, d), lambda i, k: (i, 0)),
                   pl.BlockSpec((tm, d), lambda i, k: (i, 0))],
        out_shape=[jax.ShapeDtypeStruct((t, d), F32), jax.ShapeDtypeStruct((t, d), BF16)],
        compiler_params=_params(("parallel", "arbitrary")),
    )(*args)


def _attn_kernel(q_ref, kc_ref, kp_ref, vc_ref, vp_ref, o_ref, l_ref, *, tq, scale):
    n = pl.program_id(2)
    blk = ATTN_BLOCK
    row = lax.broadcasted_iota(jnp.int32, (blk, blk), 0)
    col = lax.broadcasted_iota(jnp.int32, (blk, blk), 1)
    in_prev = col >= row
    in_cur = col <= row
    first_pen = jnp.where(n > 0, 0.0, -jnp.inf)
    nt = (((1,), (1,)), ((), ()))
    for h in range(ATTN_HPG):
        hs = slice(h * HEAD_DIM, (h + 1) * HEAD_DIM)
        for i in range(tq // blk):
            rs = slice(i * blk, (i + 1) * blk)
            q = q_ref[rs, hs]
            kc = kc_ref[rs, hs]
            vc = vc_ref[rs, hs]
            if i == 0:
                kp = kp_ref[:, hs]
                vp = vp_ref[:, hs]
            else:
                ps = slice((i - 1) * blk, i * blk)
                kp = kc_ref[ps, hs]
                vp = vc_ref[ps, hs]
            s_c = lax.dot_general(q, kc, nt, preferred_element_type=F32) * scale
            s_p = lax.dot_general(q, kp, nt, preferred_element_type=F32) * scale
            s_c = jnp.where(in_cur, s_c, -jnp.inf)
            s_p = jnp.where(in_prev, s_p, -jnp.inf)
            if i == 0:
                s_p = s_p + first_pen
            m = jnp.maximum(jnp.max(s_c, axis=-1, keepdims=True), jnp.max(s_p, axis=-1, keepdims=True))
            w_c = jnp.exp(s_c - m)
            w_p = jnp.exp(s_p - m)
            den = jnp.sum(w_c, axis=-1, keepdims=True) + jnp.sum(w_p, axis=-1, keepdims=True)
            pv = (jnp.dot(w_c.astype(BF16), vc, preferred_element_type=F32)
                  + jnp.dot(w_p.astype(BF16), vp, preferred_element_type=F32))
            o_ref[rs, hs] = pv / den
            l_ref[rs, hs] = jnp.broadcast_to(m + jnp.log(den), (blk, HEAD_DIM))


def _attn_group(qkv, g, dil):
    b, s, _ = qkv.shape
    sub = s // dil
    tq = _tile(sub, 512)
    view = qkv.reshape(b, sub, dil * 3 * D_ATTN)
    per_tok = 3 * D_ATTN // D_AGRP
    k_off = D_ATTN // D_AGRP
    v_off = 2 * k_off
    per = tq // ATTN_BLOCK

    def cur(off):
        return pl.BlockSpec((None, tq, D_AGRP), lambda bi, r, n: (bi, n, r * per_tok + off + g))

    def prev(off):
        return pl.BlockSpec((None, ATTN_BLOCK, D_AGRP),
                            lambda bi, r, n: (bi, jnp.maximum(n * per - 1, 0), r * per_tok + off + g))

    out_sds = jax.ShapeDtypeStruct((b, sub, dil * D_AGRP), F32)
    out_spec = pl.BlockSpec((None, tq, D_AGRP), lambda bi, r, n: (bi, n, r))
    o, l = pl.pallas_call(
        functools.partial(_attn_kernel, tq=tq, scale=HEAD_DIM ** -0.5),
        grid=(b, dil, sub // tq),
        in_specs=[cur(0), cur(k_off), prev(k_off), cur(v_off), prev(v_off)],
        out_specs=[out_spec, out_spec],
        out_shape=[out_sds, out_sds],
        compiler_params=_params(("parallel", "parallel", "arbitrary")),
    )(view, view, view, view, view)
    return o.reshape(b * s, D_AGRP), l.reshape(b * s, D_AGRP)


def _attn_mix_kernel(o0, o1, o2, l0, l1, l2, y_ref):
    ls = (l0[...], l1[...], l2[...])
    m = jnp.maximum(jnp.maximum(ls[0], ls[1]), ls[2])
    es = [jnp.exp(l - m) for l in ls]
    inv = 1.0 / (es[0] + es[1] + es[2])
    for g, o in enumerate((o0, o1, o2)):
        y_ref[:, g * D_AGRP:(g + 1) * D_AGRP] = (o[...] * (es[g] * inv)).astype(y_ref.dtype)


def _attn_mix(os_, ls_):
    t = os_[0].shape[0]
    tm = _tile(t, 512)
    spec = pl.BlockSpec((tm, D_AGRP), lambda i: (i, 0))
    return pl.pallas_call(
        _attn_mix_kernel,
        grid=(t // tm,),
        in_specs=[spec] * 6,
        out_specs=pl.BlockSpec((tm, D_ATTN), lambda i: (i, 0)),
        out_shape=jax.ShapeDtypeStruct((t, D_ATTN), BF16),
        compiler_params=_params(("parallel",)),
    )(*os_, *ls_)


def _shift_rows(cur, halo, sh):
    rolled = pltpu.roll(cur, sh, 0)
    hr = pltpu.roll(halo, sh, 0)
    row = lax.broadcasted_iota(jnp.int32, halo.shape, 0)
    first = jnp.where(row < sh, hr, rolled[:SUBLANES])
    return jnp.concatenate([first, rolled[SUBLANES:]], axis=0)


def _causal_conv(cur, halo, w):
    kk = w.shape[0]
    acc = w[kk - 1:kk] * cur
    for sh in range(1, kk):
        acc = acc + w[kk - 1 - sh:kk - sh] * _shift_rows(cur, halo, sh)
    return acc


def _sc_kernel(b_ref, c_ref, h_ref, ch_ref, hh_ref, w_ref, o_ref):
    s = pl.program_id(1)
    g = c_ref[...] * h_ref[...]
    gh = ch_ref[...] * hh_ref[...] * jnp.where(s > 0, 1.0, 0.0)
    o_ref[...] = (b_ref[...] * _causal_conv(g, gh, w_ref[...])).astype(o_ref.dtype)


def _short_conv(rest, conv_w):
    b, s, _ = rest.shape
    ts = _tile(s, 512)
    blk0 = R_SC // D_SC
    per = ts // SUBLANES

    def halo_rows(si):
        return jnp.maximum(si * per - 1, 0)

    def cur(j):
        return pl.BlockSpec((None, ts, D_SC), lambda bi, si: (bi, si, blk0 + j))

    def halo(j):
        return pl.BlockSpec((None, SUBLANES, D_SC), lambda bi, si: (bi, halo_rows(si), blk0 + j))

    return pl.pallas_call(
        _sc_kernel,
        grid=(b, s // ts),
        in_specs=[cur(0), cur(1), cur(2), halo(1), halo(2),
                  pl.BlockSpec((SC_CONV, D_SC), lambda bi, si: (0, 0))],
        out_specs=pl.BlockSpec((None, ts, D_SC), lambda bi, si: (bi, si, 0)),
        out_shape=jax.ShapeDtypeStruct((b, s, D_SC), BF16),
        compiler_params=_params(("parallel", "arbitrary")),
    )(rest, rest, rest, rest, rest, conv_w)


def _ssd_kernel(xs_ref, z_ref, b_ref, c_ref, dt_ref,
                wx_ref, bx_ref, wb_ref, bb_ref, wc_ref, bc_ref,
                dtb_ref, alog_ref, dsk_ref, nw_ref,
                o_ref,
                h_ref, hx_ref, hb_ref, hc_ref, xc_ref, bcv_ref, ccv_ref, dtg_ref, y_ref, *, tc):
    g = pl.program_id(1)
    s = pl.program_id(2)
    q = SSM_CHUNK
    p = SSM_HEAD_DIM

    @pl.when(s == 0)
    def _():
        h_ref[...] = jnp.zeros_like(h_ref)
        hx_ref[...] = jnp.zeros_like(hx_ref)
        hb_ref[...] = jnp.zeros_like(hb_ref)
        hc_ref[...] = jnp.zeros_like(hc_ref)

    for raw_ref, halo_ref, w_ref, bias_ref, dst_ref in (
            (xs_ref, hx_ref, wx_ref, bx_ref, xc_ref),
            (b_ref, hb_ref, wb_ref, bb_ref, bcv_ref),
            (c_ref, hc_ref, wc_ref, bc_ref, ccv_ref)):
        raw = raw_ref[...]
        pre = _causal_conv(raw, halo_ref[...], w_ref[...]) + bias_ref[...]
        dst_ref[...] = (pre * jax.nn.sigmoid(pre)).astype(dst_ref.dtype)
        halo_ref[...] = raw[tc - SUBLANES:, :]

    dt_raw = pltpu.roll(dt_ref[...], (LANES - g * SSM_HPG) % LANES, 1)
    dtg_ref[...] = jax.nn.softplus(dt_raw + dtb_ref[...])

    a_row = -jnp.exp(alog_ref[...])
    dskip = dsk_ref[...]
    li = lax.broadcasted_iota(jnp.int32, (q, q), 0)
    si = lax.broadcasted_iota(jnp.int32, (q, q), 1)
    causal = li >= si
    tri = jnp.where(causal, 1.0, 0.0).astype(F32)
    nt = (((1,), (1,)), ((), ()))

    def chunk(ci, carry):
        r0 = pl.multiple_of(ci * q, q)
        xq = xc_ref[pl.ds(r0, q), :]
        bq = bcv_ref[pl.ds(r0, q), :]
        cq = ccv_ref[pl.ds(r0, q), :]
        dtq = dtg_ref[pl.ds(r0, q), :]
        cs = jnp.dot(tri, dtq * a_row, precision=lax.Precision.HIGHEST,
                     preferred_element_type=F32)
        cs_t = cs.T
        dt_t = dtq.T
        bb = bq.astype(BF16)
        cb16 = cq.astype(BF16)
        cb = lax.dot_general(cb16, bb, nt, preferred_element_type=F32)
        b_t = bq.T
        hprev = h_ref[...]
        y_off = jnp.dot(cb16, hprev.astype(BF16), preferred_element_type=F32)
        for j in range(SSM_HPG):
            cols = slice(j * p, (j + 1) * p)
            col = cs[:, j:j + 1]
            row = cs_t[j:j + 1, :]
            dtrow = dt_t[j:j + 1, :]
            decay = jnp.exp(jnp.where(causal, col - row, -jnp.inf))
            m = (cb * decay * dtrow).astype(BF16)
            xj = xq[:, cols]
            xj16 = xj.astype(BF16)
            y_d = jnp.dot(m, xj16, preferred_element_type=F32)
            yj = y_d + y_off[:, cols] * jnp.exp(col) + dskip[:, cols] * xj
            y_ref[pl.ds(r0, q), cols] = yj
            last = cs[q - 1:q, j:j + 1]
            w_row = jnp.exp(last - row) * dtrow
            st = jnp.dot((b_t * w_row).astype(BF16), xj16, preferred_element_type=F32)
            h_ref[:, cols] = hprev[:, cols] * jnp.exp(last) + st
        return carry

    lax.fori_loop(0, tc // q, chunk, 0)

    z = z_ref[...]
    yv = y_ref[...] * (z * jax.nn.sigmoid(z))
    ms = jnp.mean(jnp.square(yv), axis=-1, keepdims=True)
    o_ref[...] = (yv * lax.rsqrt(ms + RMS_EPS) * nw_ref[...]).astype(o_ref.dtype)


def _ssd(rest, conv_w, conv_b, dt_bias, a_log, d_skip, norm_w):
    b, s, _ = rest.shape
    tc = _tile(s, 1024)
    gw, ns, hpg = SSM_GW, SSM_STATE, SSM_HPG

    def tok(width, blk):
        return pl.BlockSpec((None, tc, width), lambda bi, g, si: (bi, si, blk(g)))

    def par(rows, width, blk):
        return pl.BlockSpec((rows, width), lambda bi, g, si: (0, blk(g)))

    def grp(arr):
        a = arr.reshape(SSM_GROUPS, 1, hpg).astype(F32)
        return jnp.pad(a, ((0, 0), (0, 0), (0, LANES - hpg)))

    grp_spec = pl.BlockSpec((None, 1, LANES), lambda bi, g, si: (g, 0, 0))
    wx, wb, wc = conv_w[:, :D_SSM], conv_w[:, D_SSM:D_SSM + 4 * ns], conv_w[:, D_SSM + 4 * ns:]
    cb2 = conv_b.reshape(1, -1)
    bx, bb, bc = cb2[:, :D_SSM], cb2[:, D_SSM:D_SSM + 4 * ns], cb2[:, D_SSM + 4 * ns:]
    d_exp = jnp.repeat(d_skip.astype(F32), SSM_HEAD_DIM).reshape(1, D_SSM)
    return pl.pallas_call(
        functools.partial(_ssd_kernel, tc=tc),
        grid=(b, SSM_GROUPS, s // tc),
        in_specs=[
            tok(gw, lambda g: R_XS // gw + g),
            tok(gw, lambda g: R_Z // gw + g),
            tok(ns, lambda g: R_B // ns + g),
            tok(ns, lambda g: R_C // ns + g),
            tok(LANES, lambda g: R_DT // LANES),
            par(SSM_CONV, gw, lambda g: g), par(1, gw, lambda g: g),
            par(SSM_CONV, ns, lambda g: g), par(1, ns, lambda g: g),
            par(SSM_CONV, ns, lambda g: g), par(1, ns, lambda g: g),
            grp_spec, grp_spec,
            par(1, gw, lambda g: g), par(1, gw, lambda g: g),
        ],
        out_specs=pl.BlockSpec((None, tc, gw), lambda bi, g, si: (bi, si, g)),
        out_shape=jax.ShapeDtypeStruct((b, s, D_SSM), BF16),
        scratch_shapes=[
            pltpu.VMEM((ns, gw), F32),
            pltpu.VMEM((SUBLANES, gw), F32),
            pltpu.VMEM((SUBLANES, ns), F32),
            pltpu.VMEM((SUBLANES, ns), F32),
            pltpu.VMEM((tc, gw), F32),
            pltpu.VMEM((tc, ns), F32),
            pltpu.VMEM((tc, ns), F32),
            pltpu.VMEM((tc, LANES), F32),
            pltpu.VMEM((tc, gw), F32),
        ],
        compiler_params=_params(("parallel", "parallel", "arbitrary")),
    )(rest, rest, rest, rest, rest,
      wx, bx, wb, bb, wc, bc,
      grp(dt_bias), grp(a_log), d_exp, norm_w.reshape(1, D_SSM).astype(F32))


def _rope_tables(s_len):
    pos = jnp.arange(s_len, dtype=jnp.int32)
    inv_freq = ROPE_THETA ** (-jnp.arange(ROPE_HALF, dtype=F32) / ROPE_HALF)
    ang = pos.astype(F32)[:, None] * inv_freq[None, :]
    cos, sin = jnp.cos(ang), jnp.sin(ang)
    zeros = jnp.zeros((s_len, HEAD_DIM - 2 * ROPE_HALF), F32)
    z16 = jnp.zeros((s_len, ROPE_HALF), F32)
    c = jnp.concatenate([cos, cos, jnp.ones_like(zeros)], axis=1)
    s1 = jnp.concatenate([-sin, z16, zeros], axis=1)
    s2 = jnp.concatenate([z16, sin, zeros], axis=1)
    rep = D_AGRP // HEAD_DIM
    return tuple(jnp.tile(t, (1, rep)) for t in (c, s1, s2))


def _split_w_in(w):
    sizes = (D_ATTN, D_ATTN, D_ATTN, D_SSM, D_SSM + 2 * SSM_GROUPS * SSM_STATE, SSM_HEADS, D_SC, D_SC, D_SC)
    offs = [0]
    for sz in sizes:
        offs.append(offs[-1] + sz)
    q0, z0, xbc0, dt0, sc0 = offs[0], offs[3], offs[4], offs[5], offs[6]
    w_qkv = w[:, q0:z0]
    pad = jnp.zeros((w.shape[0], LANES - SSM_HEADS), w.dtype)
    w_rest = jnp.concatenate([
        w[:, xbc0:xbc0 + D_SSM], w[:, z0:xbc0], w[:, xbc0 + D_SSM:dt0],
        w[:, sc0:offs[9]], w[:, dt0:sc0], pad], axis=1)
    return w_qkv.astype(BF16), w_rest.astype(BF16)


def kernel(x, p, w_in, ssm_conv_w, ssm_conv_b, ssm_dt_bias, ssm_a_log, ssm_d, ssm_norm_w, sc_conv_w,
           w_out, ln1_g, ln1_b, w_up, w_down, ln2_g, ln2_b, w_pe, w_gate, ln3_g, ln3_b):
    bsz, s_len, d_model = x.shape
    depth = w_in.shape[0]
    t = bsz * s_len
    alpha = (2.0 * depth) ** 0.25
    tm = _tile(s_len, 1024)
    rope = _rope_tables(s_len)
    rope_blocks = s_len // tm
    rope_specs = [pl.BlockSpec((tm, D_AGRP), lambda i, j: (i % rope_blocks, 0))] * 3
    n_rope_tiles = 2 * D_ATTN // D_AGRP

    xf = x.reshape(t, d_model)
    xb = xf.astype(BF16)
    for i in range(depth):
        w_qkv, w_rest = _split_w_in(w_in[i])
        qkv = _mm_kfull(functools.partial(_qkv_kernel, n_rope_tiles=n_rope_tiles),
                        xb, w_qkv, BF16, tm, D_AGRP, rope, rope_specs)
        rest = _mm_kfull(_plain_kernel, xb, w_rest, F32, tm, R_TN)
        qkv3 = qkv.reshape(bsz, s_len, 3 * D_ATTN)
        outs = [_attn_group(qkv3, g, dil) for g, (_, dil) in enumerate(ATTN_GROUPS)]
        y_attn = _attn_mix([o for o, _ in outs], [l for _, l in outs])
        rest3 = rest.reshape(bsz, s_len, R_WIDTH)
        y_ssm = _ssd(rest3, ssm_conv_w[i], ssm_conv_b[i], ssm_dt_bias[i], ssm_a_log[i], ssm_d[i],
                     ssm_norm_w[i]).reshape(t, D_SSM)
        y_sc = _short_conv(rest3, sc_conv_w[i]).reshape(t, D_SC)
        mixed_in = jnp.concatenate([y_attn, y_ssm, y_sc], axis=1)
        xf, xb = _mm_ln(mixed_in, w_out[i].astype(BF16), xf, ln1_g[i], ln1_b[i], alpha)
        hid = _mm_kfull(_relu2_kernel, xb, w_up[i].astype(BF16), BF16, tm, 512)
        xf, xb = _mm_ln(hid, w_down[i].astype(BF16), xf, ln2_g[i], ln2_b[i], alpha)
        xf, xb = _mm_ln(xb, w_gate[i].astype(BF16), xf, ln3_g[i], ln3_b[i], alpha,
                        p=p[i].reshape(t, -1), w_pe=w_pe[i].astype(BF16))
    return xf.reshape(bsz, s_len, d_model)
```

```python
import functools

import jax
import jax.numpy as jnp
from jax import lax
from jax.experimental import pallas as pl
from jax.experimental.pallas import tpu as pltpu

HEAD_DIM = 128
ATTN_DILATIONS = (1, 4, 16)
ATTN_HPG = 4
ATTN_BLOCK = 128
D_ATTN = ATTN_HPG * len(ATTN_DILATIONS) * HEAD_DIM
D_AGRP = ATTN_HPG * HEAD_DIM
ROPE_THETA = 500000.0
ROPE_HALF = HEAD_DIM // 8
D_SSM = 1536
SSM_HEAD_DIM = 64
SSM_HEADS = D_SSM // SSM_HEAD_DIM
SSM_GROUPS = 4
SSM_HPG = SSM_HEADS // SSM_GROUPS
SSM_GW = D_SSM // SSM_GROUPS
SSM_STATE = 128
SSM_CONV = 4
SSM_CHUNK = 128
D_SC = 1024
SC_CONV = 3
LN_EPS = 1e-5
RMS_EPS = 1e-5

U_Z = 3 * D_ATTN
U_DT = U_Z + 2 * D_SSM + 2 * SSM_GROUPS * SSM_STATE
U_SC = U_DT + SSM_HEADS
D_ZX = U_DT - U_Z
ZX_XS = D_SSM
ZX_B = 2 * D_SSM
ZX_C = ZX_B + SSM_GROUPS * SSM_STATE

LANES = 128
SUBLANES = 8
VMEM_LIMIT_BYTES = 56 * 1024 * 1024

BF16 = jnp.bfloat16
F32 = jnp.float32

MM_TM = 1024
MM_TN = 512
LN_TM = 256
LN_ROW_CHUNK = 32


def _params(sem):
    return pltpu.CompilerParams(dimension_semantics=sem, vmem_limit_bytes=VMEM_LIMIT_BYTES)


def _tile(n, pref):
    t = min(n, pref)
    while n % t:
        t -= 1
    return t


def _log2(n):
    assert n > 0 and n & (n - 1) == 0, n
    return n.bit_length() - 1


def _mmw_kernel(*refs, n_x, mode, alpha, w_shift, w_t):
    xs, ws = refs[:n_x], refs[n_x:2 * n_x]
    pos = 2 * n_x
    if w_shift:
        w_next_ref = refs[pos]
        pos += 1
    if mode == "rope":
        c_ref, s1_ref, s2_ref = refs[pos:pos + 3]
        pos += 3
    elif mode == "resid":
        r_ref = refs[pos]
        pos += 1
    elif mode == "gate":
        r_ref, p_ref, wpe_ref = refs[pos:pos + 3]
        pos += 3
    o_ref = refs[pos]
    wbs = refs[pos + 1:pos + 1 + n_x]

    @pl.when(pl.program_id(1) == 0)
    def _():
        if w_shift:
            keep = wbs[0].shape[0] - w_shift
            wbs[0][:keep, :] = ws[0][w_shift:, :].astype(BF16)
            wbs[0][keep:, :] = w_next_ref[:w_shift, :].astype(BF16)
        else:
            for w_ref, wb_ref in zip(ws, wbs):
                wb_ref[...] = w_ref[...].astype(BF16)

    contract = (((1,), (1,)), ((), ())) if w_t else (((1,), (0,)), ((), ()))
    acc = None
    for x_ref, wb_ref in zip(xs, wbs):
        part = lax.dot_general(x_ref[...], wb_ref[...], contract, preferred_element_type=F32)
        acc = part if acc is None else acc + part

    if mode == "rope":
        tn = acc.shape[1]
        rep = tn // HEAD_DIM
        acc = (acc * jnp.tile(c_ref[...], (1, rep))
               + pltpu.roll(acc, tn - ROPE_HALF, 1) * jnp.tile(s1_ref[...], (1, rep))
               + pltpu.roll(acc, ROPE_HALF, 1) * jnp.tile(s2_ref[...], (1, rep)))
    elif mode == "relu2":
        acc = jnp.square(jnp.maximum(acc, 0.0))
    elif mode == "resid":
        acc = alpha * r_ref[...] + acc
    elif mode == "gate":
        emb = jnp.dot(p_ref[...].astype(BF16), wpe_ref[...].astype(BF16), preferred_element_type=F32)
        acc = alpha * r_ref[...] + jax.nn.sigmoid(acc) * emb
    o_ref[...] = acc.astype(o_ref.dtype)


def _mmw(xs, w_specs, layer, n, out_dtype, mode="plain", extra_inputs=(), extra_specs=(), alpha=1.0, tn=MM_TN,
         w_shift=0, w_t=False):
    t = xs[0].shape[0]
    tm = _tile(t, MM_TM)
    assert n % tn == 0
    in_specs = [pl.BlockSpec((tm, x.shape[1]), lambda j, i: (i, 0)) for x in xs]
    w_arrays = [w for w, _, _, _ in w_specs]

    def w_spec(kdim, kblk, cfn, off=0, **kw):
        if w_t:
            return pl.BlockSpec((None, tn, kdim), lambda j, i: (layer, cfn(j) + off, kblk), **kw)
        return pl.BlockSpec((None, kdim, tn), lambda j, i: (layer, kblk, cfn(j) + off), **kw)

    for _, kdim, kblk, cfn in w_specs:
        in_specs.append(w_spec(kdim, kblk, cfn))
    if w_shift:
        assert w_t and len(w_specs) == 1 and w_shift % SUBLANES == 0
        w, kdim, kblk, cfn = w_specs[0]
        in_specs.append(w_spec(kdim, kblk, cfn, off=1, pipeline_mode=pl.Buffered(1)))
        w_arrays.append(w)
    return pl.pallas_call(
        functools.partial(_mmw_kernel, n_x=len(xs), mode=mode, alpha=alpha, w_shift=w_shift, w_t=w_t),
        grid=(n // tn, t // tm),
        in_specs=in_specs + list(extra_specs),
        out_specs=pl.BlockSpec((tm, tn), lambda j, i: (i, j)),
        out_shape=jax.ShapeDtypeStruct((t, n), out_dtype),
        scratch_shapes=[pltpu.VMEM((tn, kdim) if w_t else (kdim, tn), BF16) for _, kdim, _, _ in w_specs],
        compiler_params=_params(("arbitrary", "arbitrary")),
    )(*xs, *w_arrays, *extra_inputs)


def _mmk_kernel(x_ref, w_ref, r_ref, o_ref, acc_ref, *, alpha):
    k = pl.program_id(2)

    @pl.when(k == 0)
    def _():
        acc_ref[...] = jnp.zeros_like(acc_ref)

    acc_ref[...] += jnp.dot(x_ref[...], w_ref[...], preferred_element_type=F32)

    @pl.when(k == pl.num_programs(2) - 1)
    def _():
        o_ref[...] = alpha * r_ref[...] + acc_ref[...]


def _mmk_resid(x, w, resid, alpha):
    t, kdim = x.shape
    n = w.shape[1]
    tm, tn, tk = _tile(t, 1024), _tile(n, 1024), _tile(kdim, 2048)
    return pl.pallas_call(
        functools.partial(_mmk_kernel, alpha=alpha),
        grid=(t // tm, n // tn, kdim // tk),
        in_specs=[pl.BlockSpec((tm, tk), lambda i, j, k: (i, k)),
                  pl.BlockSpec((tk, tn), lambda i, j, k: (k, j)),
                  pl.BlockSpec((tm, tn), lambda i, j, k: (i, j))],
        out_specs=pl.BlockSpec((tm, tn), lambda i, j, k: (i, j)),
        out_shape=jax.ShapeDtypeStruct((t, n), F32),
        scratch_shapes=[pltpu.VMEM((tm, tn), F32)],
        compiler_params=_params(("parallel", "parallel", "arbitrary")),
    )(x, w, resid)


def _stream_perm(tm, dil, inverse):
    n = tm // dil
    a = lax.broadcasted_iota(jnp.int32, (tm, tm), 0)
    b = lax.broadcasted_iota(jnp.int32, (tm, tm), 1)
    if inverse:
        src = jnp.bitwise_and(a, dil - 1) * n + lax.shift_right_logical(a, _log2(dil))
    else:
        src = jnp.bitwise_and(a, n - 1) * dil + lax.shift_right_logical(a, _log2(n))
    return b == src


def _ln_kernel(*refs, normalize, emit_f32, emit_bf16, emit_streams):
    y_ref, g_ref, b_ref = refs[:3]
    outs = list(refs[3:])
    f32_ref = outs.pop(0) if emit_f32 else None
    bf16_ref = outs.pop(0) if emit_bf16 else None
    assert emit_bf16 or not emit_streams
    tm = y_ref.shape[0]
    rows = _tile(tm, LN_ROW_CHUNK)

    def norm_rows(ri, carry):
        rs = pl.ds(pl.multiple_of(ri * rows, rows), rows)
        y = y_ref[rs, :]
        if normalize:
            mu = jnp.mean(y, axis=-1, keepdims=True)
            yc = y - mu
            var = jnp.mean(jnp.square(yc), axis=-1, keepdims=True)
            y = yc * lax.rsqrt(var + LN_EPS) * g_ref[...] + b_ref[...]
        if emit_f32:
            f32_ref[rs, :] = y
        if emit_bf16:
            bf16_ref[rs, :] = y.astype(BF16)
        return carry

    lax.fori_loop(0, tm // rows, norm_rows, 0)

    if emit_streams:
        y16 = bf16_ref[...]
        for dil in ATTN_DILATIONS[1:]:
            o_ref = outs.pop(0)
            perm = jnp.where(_stream_perm(tm, dil, inverse=False), 1.0, 0.0).astype(BF16)
            moved = jnp.dot(perm, y16, preferred_element_type=F32).astype(BF16)
            n = tm // dil
            for r in range(dil):
                o_ref[r] = moved[r * n:(r + 1) * n]


def _layer_norm(y3, gain, bias, normalize=True, emit_f32=True, emit_bf16=True, emit_streams=False):
    b, s, d = y3.shape
    tm = _tile(s, LN_TM)
    tok = pl.BlockSpec((None, tm, d), lambda bi, i: (bi, i, 0))
    vec = pl.BlockSpec((1, d), lambda bi, i: (0, 0))
    out_specs, out_shape = [], []
    if emit_f32:
        out_specs.append(tok)
        out_shape.append(jax.ShapeDtypeStruct((b, s, d), F32))
    if emit_bf16:
        out_specs.append(tok)
        out_shape.append(jax.ShapeDtypeStruct((b, s, d), BF16))
    if emit_streams:
        for dil in ATTN_DILATIONS[1:]:
            out_specs.append(pl.BlockSpec((None, dil, tm // dil, d), lambda bi, i: (bi, 0, i, 0)))
            out_shape.append(jax.ShapeDtypeStruct((b, dil, s // dil, d), BF16))
    return pl.pallas_call(
        functools.partial(_ln_kernel, normalize=normalize, emit_f32=emit_f32, emit_bf16=emit_bf16,
                          emit_streams=emit_streams),
        grid=(b, s // tm),
        in_specs=[tok, vec, vec],
        out_specs=out_specs,
        out_shape=out_shape,
        compiler_params=_params(("parallel", "parallel")),
    )(y3, gain.reshape(1, d), bias.reshape(1, d))


def _attn_kernel(q_ref, kc_ref, kp_ref, vc_ref, vp_ref, o_ref, l_ref, *, tq, scale):
    n = pl.program_id(2)
    blk = ATTN_BLOCK
    row = lax.broadcasted_iota(jnp.int32, (blk, blk), 0)
    col = lax.broadcasted_iota(jnp.int32, (blk, blk), 1)
    in_prev = col >= row
    in_cur = col <= row
    first_pen = jnp.where(n > 0, 0.0, -jnp.inf)
    nt = (((1,), (1,)), ((), ()))
    for h in range(ATTN_HPG):
        hs = slice(h * HEAD_DIM, (h + 1) * HEAD_DIM)
        for i in range(tq // blk):
            rs = slice(i * blk, (i + 1) * blk)
            q = q_ref[rs, hs]
            kc = kc_ref[rs, hs]
            vc = vc_ref[rs, hs]
            if i == 0:
                kp = kp_ref[:, hs]
                vp = vp_ref[:, hs]
            else:
                ps = slice((i - 1) * blk, i * blk)
                kp = kc_ref[ps, hs]
                vp = vc_ref[ps, hs]
            s_c = lax.dot_general(q, kc, nt, preferred_element_type=F32) * scale
            s_p = lax.dot_general(q, kp, nt, preferred_element_type=F32) * scale
            s_c = jnp.where(in_cur, s_c, -jnp.inf)
            s_p = jnp.where(in_prev, s_p, -jnp.inf)
            if i == 0:
                s_p = s_p + first_pen
            m = jnp.maximum(jnp.max(s_c, axis=-1, keepdims=True), jnp.max(s_p, axis=-1, keepdims=True))
            w_c = jnp.exp(s_c - m)
            w_p = jnp.exp(s_p - m)
            den = jnp.sum(w_c, axis=-1, keepdims=True) + jnp.sum(w_p, axis=-1, keepdims=True)
            pv = (jnp.dot(w_c.astype(BF16), vc, preferred_element_type=F32)
                  + jnp.dot(w_p.astype(BF16), vp, preferred_element_type=F32))
            o_ref[rs, hs] = pv / den
            l_ref[rs, hs] = jnp.broadcast_to(m + jnp.log(den), (blk, HEAD_DIM))


def _attn_group(qkv):
    b, dil, sub, _ = qkv.shape
    tq = _tile(sub, 512)
    per = tq // ATTN_BLOCK

    def cur(c):
        return pl.BlockSpec((None, None, tq, D_AGRP), lambda bi, r, n: (bi, r, n, c))

    def prev(c):
        return pl.BlockSpec((None, None, ATTN_BLOCK, D_AGRP),
                            lambda bi, r, n: (bi, r, jnp.maximum(n * per - 1, 0), c))

    out_sds = jax.ShapeDtypeStruct((b, dil, sub, D_AGRP), F32)
    return pl.pallas_call(
        functools.partial(_attn_kernel, tq=tq, scale=HEAD_DIM ** -0.5),
        grid=(b, dil, sub // tq),
        in_specs=[cur(0), cur(1), prev(1), cur(2), prev(2)],
        out_specs=[cur(0), cur(0)],
        out_shape=[out_sds, out_sds],
        compiler_params=_params(("parallel", "parallel", "arbitrary")),
    )(qkv, qkv, qkv, qkv, qkv)


def _to_token_order(x_ref, dil):
    if dil == 1:
        return x_ref[0]
    n, w = x_ref.shape[1], x_ref.shape[2]
    x = x_ref[...].reshape(dil * n, w)
    perm = jnp.where(_stream_perm(dil * n, dil, inverse=True), 1.0, 0.0).astype(BF16)
    hi = x.astype(BF16)
    r1 = x - hi.astype(F32)
    mid = r1.astype(BF16)
    lo = (r1 - mid.astype(F32)).astype(BF16)
    return (jnp.dot(perm, hi, preferred_element_type=F32)
            + jnp.dot(perm, mid, preferred_element_type=F32)
            + jnp.dot(perm, lo, preferred_element_type=F32))


def _attn_mix_kernel(o0, o1, o2, l0, l1, l2, y_ref):
    os_ = [_to_token_order(o, d) for o, d in zip((o0, o1, o2), ATTN_DILATIONS)]
    ls = [_to_token_order(l, d) for l, d in zip((l0, l1, l2), ATTN_DILATIONS)]
    m = jnp.maximum(jnp.maximum(ls[0], ls[1]), ls[2])
    es = [jnp.exp(l - m) for l in ls]
    inv = 1.0 / (es[0] + es[1] + es[2])
    for g in range(len(ATTN_DILATIONS)):
        y_ref[:, g * D_AGRP:(g + 1) * D_AGRP] = (os_[g] * (es[g] * inv)).astype(y_ref.dtype)


def _attn_mix(os_, ls_):
    b, _, s, _ = os_[0].shape
    tm = _tile(s, LN_TM)
    specs = [pl.BlockSpec((None, d, tm // d, D_AGRP), lambda bi, i: (bi, 0, i, 0)) for d in ATTN_DILATIONS]
    return pl.pallas_call(
        _attn_mix_kernel,
        grid=(b, s // tm),
        in_specs=specs + specs,
        out_specs=pl.BlockSpec((None, tm, D_ATTN), lambda bi, i: (bi, i, 0)),
        out_shape=jax.ShapeDtypeStruct((b, s, D_ATTN), BF16),
        compiler_params=_params(("parallel", "parallel")),
    )(*os_, *ls_)


def _shift_rows(cur, halo, sh):
    rolled = pltpu.roll(cur, sh, 0)
    hr = pltpu.roll(halo, sh, 0)
    row = lax.broadcasted_iota(jnp.int32, halo.shape, 0)
    first = jnp.where(row < sh, hr, rolled[:SUBLANES])
    return jnp.concatenate([first, rolled[SUBLANES:]], axis=0)


def _causal_conv(cur, halo, w):
    kk = w.shape[0]
    acc = w[kk - 1:kk] * cur
    for sh in range(1, kk):
        acc = acc + w[kk - 1 - sh:kk - sh] * _shift_rows(cur, halo, sh)
    return acc


def _sc_kernel(b_ref, c_ref, h_ref, ch_ref, hh_ref, w_ref, o_ref):
    s = pl.program_id(1)
    g = c_ref[...] * h_ref[...]
    gh = ch_ref[...] * hh_ref[...] * jnp.where(s > 0, 1.0, 0.0)
    o_ref[...] = (b_ref[...] * _causal_conv(g, gh, w_ref[...])).astype(o_ref.dtype)


def _short_conv(sc, conv_w):
    b, s, _ = sc.shape
    ts = _tile(s, 512)
    per = ts // SUBLANES

    def cur(j):
        return pl.BlockSpec((None, ts, D_SC), lambda bi, si: (bi, si, j))

    def halo(j):
        return pl.BlockSpec((None, SUBLANES, D_SC), lambda bi, si: (bi, jnp.maximum(si * per - 1, 0), j))

    return pl.pallas_call(
        _sc_kernel,
        grid=(b, s // ts),
        in_specs=[cur(0), cur(1), cur(2), halo(1), halo(2),
                  pl.BlockSpec((SC_CONV, D_SC), lambda bi, si: (0, 0))],
        out_specs=pl.BlockSpec((None, ts, D_SC), lambda bi, si: (bi, si, 0)),
        out_shape=jax.ShapeDtypeStruct((b, s, D_SC), BF16),
        compiler_params=_params(("parallel", "arbitrary")),
    )(sc, sc, sc, sc, sc, conv_w)


def _ssd_kernel(xs_ref, z_ref, b_ref, c_ref, dt_ref,
                wx_ref, bx_ref, wb_ref, bb_ref, wc_ref, bc_ref,
                dtb_ref, alog_ref, dsk_ref, nw_ref,
                o_ref,
                h_ref, hx_ref, hb_ref, hc_ref, xc_ref, bcv_ref, ccv_ref, dtg_ref, y_ref, *, tc):
    g = pl.program_id(1)
    s = pl.program_id(2)
    q = SSM_CHUNK
    p = SSM_HEAD_DIM

    @pl.when(s == 0)
    def _():
        h_ref[...] = jnp.zeros_like(h_ref)
        hx_ref[...] = jnp.zeros_like(hx_ref)
        hb_ref[...] = jnp.zeros_like(hb_ref)
        hc_ref[...] = jnp.zeros_like(hc_ref)

    for raw_ref, halo_ref, w_ref, bias_ref, dst_ref in (
            (xs_ref, hx_ref, wx_ref, bx_ref, xc_ref),
            (b_ref, hb_ref, wb_ref, bb_ref, bcv_ref),
            (c_ref, hc_ref, wc_ref, bc_ref, ccv_ref)):
        raw = raw_ref[...]
        pre = _causal_conv(raw, halo_ref[...], w_ref[...]) + bias_ref[...]
        dst_ref[...] = (pre * jax.nn.sigmoid(pre)).astype(dst_ref.dtype)
        halo_ref[...] = raw[tc - SUBLANES:, :]

    dt_raw = pltpu.roll(dt_ref[...], (LANES - g * SSM_HPG) % LANES, 1)
    dtg_ref[...] = jax.nn.softplus(dt_raw + dtb_ref[...])

    a_row = -jnp.exp(alog_ref[...])
    dskip = dsk_ref[...]
    li = lax.broadcasted_iota(jnp.int32, (q, q), 0)
    si = lax.broadcasted_iota(jnp.int32, (q, q), 1)
    causal = li >= si
    tri = jnp.where(causal, 1.0, 0.0).astype(F32)
    nt = (((1,), (1,)), ((), ()))

    def chunk(ci, carry):
        r0 = pl.multiple_of(ci * q, q)
        xq = xc_ref[pl.ds(r0, q), :]
        bq = bcv_ref[pl.ds(r0, q), :]
        cq = ccv_ref[pl.ds(r0, q), :]
        dtq = dtg_ref[pl.ds(r0, q), :]
        cs = jnp.dot(tri, dtq * a_row, precision=lax.Precision.HIGHEST,
                     preferred_element_type=F32)
        cs_t = cs.T
        dt_t = dtq.T
        bb = bq.astype(BF16)
        cb16 = cq.astype(BF16)
        cb = lax.dot_general(cb16, bb, nt, preferred_element_type=F32)
        b_t = bq.T
        hprev = h_ref[...]
        y_off = jnp.dot(cb16, hprev.astype(BF16), preferred_element_type=F32)
        for j in range(SSM_HPG):
            cols = slice(j * p, (j + 1) * p)
            col = cs[:, j:j + 1]
            row = cs_t[j:j + 1, :]
            dtrow = dt_t[j:j + 1, :]
            decay = jnp.exp(jnp.where(causal, col - row, -jnp.inf))
            m = (cb * decay * dtrow).astype(BF16)
            xj = xq[:, cols]
            xj16 = xj.astype(BF16)
            y_d = jnp.dot(m, xj16, preferred_element_type=F32)
            yj = y_d + y_off[:, cols] * jnp.exp(col) + dskip[:, cols] * xj
            y_ref[pl.ds(r0, q), cols] = yj
            last = cs[q - 1:q, j:j + 1]
            w_row = jnp.exp(last - row) * dtrow
            st = jnp.dot((b_t * w_row).astype(BF16), xj16, preferred_element_type=F32)
            h_ref[:, cols] = hprev[:, cols] * jnp.exp(last) + st
        return carry

    lax.fori_loop(0, tc // q, chunk, 0)

    z = z_ref[...]
    yv = y_ref[...] * (z * jax.nn.sigmoid(z))
    ms = jnp.mean(jnp.square(yv), axis=-1, keepdims=True)
    o_ref[...] = (yv * lax.rsqrt(ms + RMS_EPS) * nw_ref[...]).astype(o_ref.dtype)


def _ssd(zx, dt, conv_w, conv_b, dt_bias, a_log, d_skip, norm_w):
    b, s, _ = zx.shape
    tc = _tile(s, 1024)
    gw, ns, hpg = SSM_GW, SSM_STATE, SSM_HPG

    def tok(width, base):
        return pl.BlockSpec((None, tc, width), lambda bi, g, si: (bi, si, base // width + g))

    def par(rows, width):
        return pl.BlockSpec((rows, width), lambda bi, g, si: (0, g))

    def grp(arr):
        a = arr.reshape(SSM_GROUPS, 1, hpg).astype(F32)
        return jnp.pad(a, ((0, 0), (0, 0), (0, LANES - hpg)))

    grp_spec = pl.BlockSpec((None, 1, LANES), lambda bi, g, si: (g, 0, 0))
    wx, wb, wc = conv_w[:, :D_SSM], conv_w[:, D_SSM:D_SSM + 4 * ns], conv_w[:, D_SSM + 4 * ns:]
    cb2 = conv_b.reshape(1, -1)
    bx, bb, bc = cb2[:, :D_SSM], cb2[:, D_SSM:D_SSM + 4 * ns], cb2[:, D_SSM + 4 * ns:]
    d_exp = jnp.repeat(d_skip.astype(F32), SSM_HEAD_DIM).reshape(1, D_SSM)
    return pl.pallas_call(
        functools.partial(_ssd_kernel, tc=tc),
        grid=(b, SSM_GROUPS, s // tc),
        in_specs=[
            tok(gw, ZX_XS), tok(gw, 0), tok(ns, ZX_B), tok(ns, ZX_C),
            pl.BlockSpec((None, tc, LANES), lambda bi, g, si: (bi, si, 0)),
            par(SSM_CONV, gw), par(1, gw),
            par(SSM_CONV, ns), par(1, ns),
            par(SSM_CONV, ns), par(1, ns),
            grp_spec, grp_spec,
            par(1, gw), par(1, gw),
        ],
        out_specs=pl.BlockSpec((None, tc, gw), lambda bi, g, si: (bi, si, g)),
        out_shape=jax.ShapeDtypeStruct((b, s, D_SSM), BF16),
        scratch_shapes=[
            pltpu.VMEM((ns, gw), F32),
            pltpu.VMEM((SUBLANES, gw), F32),
            pltpu.VMEM((SUBLANES, ns), F32),
            pltpu.VMEM((SUBLANES, ns), F32),
            pltpu.VMEM((tc, gw), F32),
            pltpu.VMEM((tc, ns), F32),
            pltpu.VMEM((tc, ns), F32),
            pltpu.VMEM((tc, LANES), F32),
            pltpu.VMEM((tc, gw), F32),
        ],
        compiler_params=_params(("parallel", "parallel", "arbitrary")),
    )(zx, zx, zx, zx, dt,
      wx, bx, wb, bb, wc, bc,
      grp(dt_bias), grp(a_log), d_exp, norm_w.reshape(1, D_SSM).astype(F32))


def _rope_tables(s_len):
    pos = jnp.arange(s_len, dtype=jnp.int32)
    inv_freq = ROPE_THETA ** (-jnp.arange(ROPE_HALF, dtype=F32) / ROPE_HALF)
    ang = pos.astype(F32)[:, None] * inv_freq[None, :]
    cos, sin = jnp.cos(ang), jnp.sin(ang)
    rest = jnp.zeros((s_len, HEAD_DIM - 2 * ROPE_HALF), F32)
    z16 = jnp.zeros((s_len, ROPE_HALF), F32)
    c = jnp.concatenate([cos, cos, jnp.ones_like(rest)], axis=1)
    s1 = jnp.concatenate([-sin, z16, rest], axis=1)
    s2 = jnp.concatenate([z16, sin, rest], axis=1)
    ident = (jnp.ones_like(c), jnp.zeros_like(c), jnp.zeros_like(c))
    return tuple(jnp.stack([t, i]) for t, i in zip((c, s1, s2), ident))


def _stream_major_rows(tab, dil):
    k, s, w = tab.shape
    return tab.reshape(k, s // dil, dil, w).transpose(0, 2, 1, 3).reshape(k, s, w)


def kernel(x, p, w_in, ssm_conv_w, ssm_conv_b, ssm_dt_bias, ssm_a_log, ssm_d, ssm_norm_w, sc_conv_w,
           w_out, ln1_g, ln1_b, w_up, w_down, ln2_g, ln2_b, w_pe, w_gate, ln3_g, ln3_b):
    bsz, s_len, d_model = x.shape
    depth = w_in.shape[0]
    t = bsz * s_len
    d_ple = p.shape[-1]
    alpha = (2.0 * depth) ** 0.25
    tm = _tile(t, MM_TM)
    assert s_len % tm == 0
    rope_blocks = s_len // tm
    rope = _rope_tables(s_len)
    rope_specs = [pl.BlockSpec((None, tm, HEAD_DIM), lambda j, i: (j // 2, i % rope_blocks, 0))] * 3
    rope_by_dil = {d: tuple(_stream_major_rows(tab, d) for tab in rope) for d in ATTN_DILATIONS}
    n_grp = len(ATTN_DILATIONS)
    w_in_t = jnp.swapaxes(w_in, 1, 2)

    def flat(a):
        return a.reshape(t, a.shape[-1])

    xf = x
    xb, xb4, xb16 = _layer_norm(x, ln1_g[0], ln1_b[0], normalize=False, emit_f32=False, emit_streams=True)
    for i in range(depth):
        streams = {1: xb, 4: xb4, 16: xb16}
        outs = []
        for g, dil in enumerate(ATTN_DILATIONS):
            qkv = _mmw([flat(streams[dil])], [(w_in_t, d_model, 0, lambda j, g=g: j * n_grp + g)], i,
                       3 * D_AGRP, BF16, mode="rope", extra_inputs=rope_by_dil[dil], extra_specs=rope_specs,
                       w_t=True)
            outs.append(_attn_group(qkv.reshape(bsz, dil, s_len // dil, 3 * D_AGRP)))
        y_attn = _attn_mix([o for o, _ in outs], [l for _, l in outs])

        xbf = flat(xb)
        zx = _mmw([xbf], [(w_in_t, d_model, 0, lambda j: U_Z // MM_TN + j)], i, D_ZX, F32, w_t=True)
        dt = _mmw([xbf], [(w_in_t, d_model, 0, lambda j: U_DT // LANES)], i, LANES, F32, tn=LANES, w_t=True)
        sc = _mmw([xbf], [(w_in_t, d_model, 0, lambda j: U_DT // MM_TN + j)], i, 3 * D_SC, F32,
                  w_shift=U_SC - U_DT, w_t=True)
        y_ssm = _ssd(zx.reshape(bsz, s_len, D_ZX), dt.reshape(bsz, s_len, LANES), ssm_conv_w[i], ssm_conv_b[i],
                     ssm_dt_bias[i], ssm_a_log[i], ssm_d[i], ssm_norm_w[i])
        y_sc = _short_conv(sc.reshape(bsz, s_len, 3 * D_SC), sc_conv_w[i])

        resid_spec = pl.BlockSpec((tm, MM_TN), lambda j, i_: (i_, j))
        y1 = _mmw([flat(y_attn), flat(y_ssm), flat(y_sc)],
                  [(w_out, D_ATTN, 0, lambda j: j), (w_out, D_SSM, 1, lambda j: j),
                   (w_out, D_SC, (D_ATTN + D_SSM) // D_SC, lambda j: j)], i,
                  d_model, F32, mode="resid", extra_inputs=(flat(xf),), extra_specs=(resid_spec,), alpha=alpha)
        xf, xb = _layer_norm(y1.reshape(bsz, s_len, d_model), ln1_g[i], ln1_b[i])

        hid = _mmw([flat(xb)], [(w_up, d_model, 0, lambda j: j)], i, w_up.shape[2], BF16, mode="relu2")
        y2 = _mmk_resid(hid, w_down[i].astype(BF16), flat(xf), alpha)
        xf, xb = _layer_norm(y2.reshape(bsz, s_len, d_model), ln2_g[i], ln2_b[i])

        y3 = _mmw([flat(xb)], [(w_gate, d_model, 0, lambda j: j)], i, d_model, F32, mode="gate",
                  extra_inputs=(flat(xf), flat(p[i]), w_pe[i]),
                  extra_specs=(resid_spec, pl.BlockSpec((tm, d_ple), lambda j, i_: (i_, 0)),
                               pl.BlockSpec((d_ple, MM_TN), lambda j, i_: (0, j))),
                  alpha=alpha)
        last = i == depth - 1
        res = _layer_norm(y3.reshape(bsz, s_len, d_model), ln3_g[i], ln3_b[i],
                          emit_bf16=not last, emit_streams=not last)
        if last:
            xf, = res
        else:
            xf, xb, xb4, xb16 = res
    return xf
```

```python
import functools

import jax
import jax.numpy as jnp
from jax import lax
from jax.experimental import pallas as pl
from jax.experimental.pallas import tpu as pltpu

HEAD_DIM = 128
ATTN_DILATIONS = (1, 4, 16)
ATTN_HPG = 4
ATTN_BLOCK = 128
D_ATTN = ATTN_HPG * len(ATTN_DILATIONS) * HEAD_DIM
D_AGRP = ATTN_HPG * HEAD_DIM
ROPE_THETA = 500000.0
ROPE_HALF = HEAD_DIM // 8
D_SSM = 1536
SSM_HEAD_DIM = 64
SSM_HEADS = D_SSM // SSM_HEAD_DIM
SSM_GROUPS = 4
SSM_HPG = SSM_HEADS // SSM_GROUPS
SSM_GW = D_SSM // SSM_GROUPS
SSM_STATE = 128
SSM_CONV = 4
SSM_CHUNK = 128
D_SC = 1024
SC_CONV = 3
LN_EPS = 1e-5
RMS_EPS = 1e-5

U_Z = 3 * D_ATTN
U_DT = U_Z + 2 * D_SSM + 2 * SSM_GROUPS * SSM_STATE
U_SC = U_DT + SSM_HEADS
D_ZX = U_DT - U_Z
ZX_XS = D_SSM
ZX_B = 2 * D_SSM
ZX_C = ZX_B + SSM_GROUPS * SSM_STATE

LANES = 128
SUBLANES = 8
VMEM_LIMIT_BYTES = 56 * 1024 * 1024

BF16 = jnp.bfloat16
F32 = jnp.float32

MM_TM = 1024
MM_TN = 512
LN_TM = 256
LN_ROW_CHUNK = 32


def _params(sem):
    return pltpu.CompilerParams(dimension_semantics=sem, vmem_limit_bytes=VMEM_LIMIT_BYTES)


def _tile(n, pref):
    t = min(n, pref)
    while n % t:
        t -= 1
    return t


def _log2(n):
    assert n > 0 and n & (n - 1) == 0, n
    return n.bit_length() - 1


def _mmw_kernel(*refs, n_x, mode, alpha, w_shift, w_t):
    xs, ws = refs[:n_x], refs[n_x:2 * n_x]
    pos = 2 * n_x
    if w_shift:
        w_next_ref = refs[pos]
        pos += 1
    if mode == "rope":
        c_ref, s1_ref, s2_ref = refs[pos:pos + 3]
        pos += 3
    elif mode == "resid":
        r_ref = refs[pos]
        pos += 1
    elif mode == "gate":
        r_ref, p_ref, wpe_ref = refs[pos:pos + 3]
        pos += 3
    o_ref = refs[pos]
    wbs = refs[pos + 1:pos + 1 + n_x]

    @pl.when(pl.program_id(1) == 0)
    def _():
        if w_shift:
            keep = wbs[0].shape[0] - w_shift
            wbs[0][:keep, :] = ws[0][w_shift:, :].astype(BF16)
            wbs[0][keep:, :] = w_next_ref[:w_shift, :].astype(BF16)
        else:
            for w_ref, wb_ref in zip(ws, wbs):
                wb_ref[...] = w_ref[...].astype(BF16)

    contract = (((1,), (1,)), ((), ())) if w_t else (((1,), (0,)), ((), ()))
    tm, tn = o_ref.shape
    halves = 2 if mode in ("rope", "gate") and tm % (2 * SUBLANES) == 0 else 1
    rows_per = tm // halves
    for hf in range(halves):
        rs = slice(hf * rows_per, (hf + 1) * rows_per)
        acc = None
        for x_ref, wb_ref in zip(xs, wbs):
            part = lax.dot_general(x_ref[rs, :], wb_ref[...], contract, preferred_element_type=F32)
            acc = part if acc is None else acc + part

        if mode == "rope":
            rep = tn // HEAD_DIM
            acc = (acc * jnp.tile(c_ref[rs, :], (1, rep))
                   + pltpu.roll(acc, tn - ROPE_HALF, 1) * jnp.tile(s1_ref[rs, :], (1, rep))
                   + pltpu.roll(acc, ROPE_HALF, 1) * jnp.tile(s2_ref[rs, :], (1, rep)))
        elif mode == "relu2":
            acc = jnp.square(jnp.maximum(acc, 0.0))
        elif mode == "resid":
            acc = alpha * r_ref[rs, :] + acc
        elif mode == "gate":
            emb = jnp.dot(p_ref[rs, :].astype(BF16), wpe_ref[...].astype(BF16), preferred_element_type=F32)
            acc = alpha * r_ref[rs, :] + jax.nn.sigmoid(acc) * emb
        o_ref[rs, :] = acc.astype(o_ref.dtype)


def _mmw(xs, w_specs, layer, n, out_dtype, mode="plain", extra_inputs=(), extra_specs=(), alpha=1.0, tn=MM_TN,
         w_shift=0, w_t=False):
    t = xs[0].shape[0]
    tm = _tile(t, MM_TM)
    assert n % tn == 0
    in_specs = [pl.BlockSpec((tm, x.shape[1]), lambda j, i: (i, 0)) for x in xs]
    w_arrays = [w for w, _, _, _ in w_specs]

    def w_spec(kdim, kblk, cfn, off=0, **kw):
        if w_t:
            return pl.BlockSpec((None, tn, kdim), lambda j, i: (layer, cfn(j) + off, kblk), **kw)
        return pl.BlockSpec((None, kdim, tn), lambda j, i: (layer, kblk, cfn(j) + off), **kw)

    for _, kdim, kblk, cfn in w_specs:
        in_specs.append(w_spec(kdim, kblk, cfn))
    if w_shift:
        assert w_t and len(w_specs) == 1 and w_shift % SUBLANES == 0
        w, kdim, kblk, cfn = w_specs[0]
        in_specs.append(w_spec(kdim, kblk, cfn, off=1, pipeline_mode=pl.Buffered(1)))
        w_arrays.append(w)
    return pl.pallas_call(
        functools.partial(_mmw_kernel, n_x=len(xs), mode=mode, alpha=alpha, w_shift=w_shift, w_t=w_t),
        grid=(n // tn, t // tm),
        in_specs=in_specs + list(extra_specs),
        out_specs=pl.BlockSpec((tm, tn), lambda j, i: (i, j)),
        out_shape=jax.ShapeDtypeStruct((t, n), out_dtype),
        scratch_shapes=[pltpu.VMEM((tn, kdim) if w_t else (kdim, tn), BF16) for _, kdim, _, _ in w_specs],
        compiler_params=_params(("arbitrary", "arbitrary")),
    )(*xs, *w_arrays, *extra_inputs)


def _mmk_kernel(x_ref, w_ref, r_ref, o_ref, acc_ref, *, alpha):
    k = pl.program_id(2)

    @pl.when(k == 0)
    def _():
        acc_ref[...] = jnp.zeros_like(acc_ref)

    acc_ref[...] += jnp.dot(x_ref[...], w_ref[...], preferred_element_type=F32)

    @pl.when(k == pl.num_programs(2) - 1)
    def _():
        o_ref[...] = alpha * r_ref[...] + acc_ref[...]


def _mmk_resid(x, w, layer, resid, alpha):
    t, kdim = x.shape
    n = w.shape[2]
    tm, tn, tk = _tile(t, 1024), _tile(n, 1024), _tile(kdim, 2048)
    return pl.pallas_call(
        functools.partial(_mmk_kernel, alpha=alpha),
        grid=(t // tm, n // tn, kdim // tk),
        in_specs=[pl.BlockSpec((tm, tk), lambda i, j, k: (i, k)),
                  pl.BlockSpec((None, tk, tn), lambda i, j, k: (layer, k, j)),
                  pl.BlockSpec((tm, tn), lambda i, j, k: (i, j))],
        out_specs=pl.BlockSpec((tm, tn), lambda i, j, k: (i, j)),
        out_shape=jax.ShapeDtypeStruct((t, n), F32),
        scratch_shapes=[pltpu.VMEM((tm, tn), F32)],
        compiler_params=_params(("parallel", "parallel", "arbitrary")),
    )(x, w, resid)


def _stream_perm(tm, dil, inverse):
    n = tm // dil
    a = lax.broadcasted_iota(jnp.int32, (tm, tm), 0)
    b = lax.broadcasted_iota(jnp.int32, (tm, tm), 1)
    if inverse:
        src = jnp.bitwise_and(a, dil - 1) * n + lax.shift_right_logical(a, _log2(dil))
    else:
        src = jnp.bitwise_and(a, n - 1) * dil + lax.shift_right_logical(a, _log2(n))
    return b == src


def _ln_kernel(*refs, normalize, emit_f32, emit_bf16, emit_streams):
    y_ref, g_ref, b_ref = refs[:3]
    outs = list(refs[3:])
    f32_ref = outs.pop(0) if emit_f32 else None
    bf16_ref = outs.pop(0) if emit_bf16 else None
    assert emit_bf16 or not emit_streams
    tm = y_ref.shape[0]
    rows = _tile(tm, LN_ROW_CHUNK)

    def norm_rows(ri, carry):
        rs = pl.ds(pl.multiple_of(ri * rows, rows), rows)
        y = y_ref[rs, :]
        if normalize:
            mu = jnp.mean(y, axis=-1, keepdims=True)
            yc = y - mu
            var = jnp.mean(jnp.square(yc), axis=-1, keepdims=True)
            y = yc * lax.rsqrt(var + LN_EPS) * g_ref[...] + b_ref[...]
        if emit_f32:
            f32_ref[rs, :] = y
        if emit_bf16:
            bf16_ref[rs, :] = y.astype(BF16)
        return carry

    lax.fori_loop(0, tm // rows, norm_rows, 0)

    if emit_streams:
        y16 = bf16_ref[...]
        for dil in ATTN_DILATIONS[1:]:
            o_ref = outs.pop(0)
            perm = jnp.where(_stream_perm(tm, dil, inverse=False), 1.0, 0.0).astype(BF16)
            moved = jnp.dot(perm, y16, preferred_element_type=F32).astype(BF16)
            n = tm // dil
            for r in range(dil):
                o_ref[r] = moved[r * n:(r + 1) * n]


def _layer_norm(y3, gain, bias, normalize=True, emit_f32=True, emit_bf16=True, emit_streams=False):
    b, s, d = y3.shape
    tm = _tile(s, LN_TM)
    tok = pl.BlockSpec((None, tm, d), lambda bi, i: (bi, i, 0))
    vec = pl.BlockSpec((1, d), lambda bi, i: (0, 0))
    out_specs, out_shape = [], []
    if emit_f32:
        out_specs.append(tok)
        out_shape.append(jax.ShapeDtypeStruct((b, s, d), F32))
    if emit_bf16:
        out_specs.append(tok)
        out_shape.append(jax.ShapeDtypeStruct((b, s, d), BF16))
    if emit_streams:
        for dil in ATTN_DILATIONS[1:]:
            out_specs.append(pl.BlockSpec((None, dil, tm // dil, d), lambda bi, i: (bi, 0, i, 0)))
            out_shape.append(jax.ShapeDtypeStruct((b, dil, s // dil, d), BF16))
    return pl.pallas_call(
        functools.partial(_ln_kernel, normalize=normalize, emit_f32=emit_f32, emit_bf16=emit_bf16,
                          emit_streams=emit_streams),
        grid=(b, s // tm),
        in_specs=[tok, vec, vec],
        out_specs=out_specs,
        out_shape=out_shape,
        compiler_params=_params(("parallel", "parallel")),
    )(y3, gain.reshape(1, d), bias.reshape(1, d))


def _attn_kernel(q_ref, kc_ref, kp_ref, vc_ref, vp_ref, o_ref, l_ref, *, tq, scale):
    n = pl.program_id(2)
    blk = ATTN_BLOCK
    row = lax.broadcasted_iota(jnp.int32, (blk, blk), 0)
    col = lax.broadcasted_iota(jnp.int32, (blk, blk), 1)
    in_prev = col >= row
    in_cur = col <= row
    first_pen = jnp.where(n > 0, 0.0, -jnp.inf)
    nt = (((1,), (1,)), ((), ()))
    for h in range(ATTN_HPG):
        hs = slice(h * HEAD_DIM, (h + 1) * HEAD_DIM)
        for i in range(tq // blk):
            rs = slice(i * blk, (i + 1) * blk)
            q = q_ref[rs, hs]
            kc = kc_ref[rs, hs]
            vc = vc_ref[rs, hs]
            if i == 0:
                kp = kp_ref[:, hs]
                vp = vp_ref[:, hs]
            else:
                ps = slice((i - 1) * blk, i * blk)
                kp = kc_ref[ps, hs]
                vp = vc_ref[ps, hs]
            s_c = lax.dot_general(q, kc, nt, preferred_element_type=F32) * scale
            s_p = lax.dot_general(q, kp, nt, preferred_element_type=F32) * scale
            s_c = jnp.where(in_cur, s_c, -jnp.inf)
            s_p = jnp.where(in_prev, s_p, -jnp.inf)
            if i == 0:
                s_p = s_p + first_pen
            m = jnp.max(jnp.maximum(s_c, s_p), axis=-1, keepdims=True)
            w_c = jnp.exp(s_c - m)
            w_p = jnp.exp(s_p - m)
            den = jnp.sum(w_c + w_p, axis=-1, keepdims=True)
            pv = (jnp.dot(w_c.astype(BF16), vc, preferred_element_type=F32)
                  + jnp.dot(w_p.astype(BF16), vp, preferred_element_type=F32))
            o_ref[rs, hs] = pv / den
            l_ref[rs, hs] = jnp.broadcast_to(m + jnp.log(den), (blk, HEAD_DIM))


def _attn_group(qkv):
    b, dil, sub, _ = qkv.shape
    tq = _tile(sub, 512)
    per = tq // ATTN_BLOCK

    def cur(c):
        return pl.BlockSpec((None, None, tq, D_AGRP), lambda bi, r, n: (bi, r, n, c))

    def prev(c):
        return pl.BlockSpec((None, None, ATTN_BLOCK, D_AGRP),
                            lambda bi, r, n: (bi, r, jnp.maximum(n * per - 1, 0), c))

    out_sds = jax.ShapeDtypeStruct((b, dil, sub, D_AGRP), F32)
    return pl.pallas_call(
        functools.partial(_attn_kernel, tq=tq, scale=HEAD_DIM ** -0.5),
        grid=(b, dil, sub // tq),
        in_specs=[cur(0), cur(1), prev(1), cur(2), prev(2)],
        out_specs=[cur(0), cur(0)],
        out_shape=[out_sds, out_sds],
        compiler_params=_params(("parallel", "parallel", "arbitrary")),
    )(qkv, qkv, qkv, qkv, qkv)


def _to_token_order(x_ref, dil):
    if dil == 1:
        return x_ref[0]
    n, w = x_ref.shape[1], x_ref.shape[2]
    x = x_ref[...].reshape(dil * n, w)
    perm = jnp.where(_stream_perm(dil * n, dil, inverse=True), 1.0, 0.0).astype(BF16)
    hi = x.astype(BF16)
    r1 = x - hi.astype(F32)
    mid = r1.astype(BF16)
    lo = (r1 - mid.astype(F32)).astype(BF16)
    return (jnp.dot(perm, hi, preferred_element_type=F32)
            + jnp.dot(perm, mid, preferred_element_type=F32)
            + jnp.dot(perm, lo, preferred_element_type=F32))


def _attn_mix_kernel(o0, o1, o2, l0, l1, l2, y_ref):
    os_ = [_to_token_order(o, d) for o, d in zip((o0, o1, o2), ATTN_DILATIONS)]
    ls = [_to_token_order(l, d) for l, d in zip((l0, l1, l2), ATTN_DILATIONS)]
    m = jnp.maximum(jnp.maximum(ls[0], ls[1]), ls[2])
    es = [jnp.exp(l - m) for l in ls]
    inv = 1.0 / (es[0] + es[1] + es[2])
    for g in range(len(ATTN_DILATIONS)):
        y_ref[:, g * D_AGRP:(g + 1) * D_AGRP] = (os_[g] * (es[g] * inv)).astype(y_ref.dtype)


def _attn_mix(os_, ls_):
    b, _, s, _ = os_[0].shape
    tm = _tile(s, LN_TM)
    specs = [pl.BlockSpec((None, d, tm // d, D_AGRP), lambda bi, i: (bi, 0, i, 0)) for d in ATTN_DILATIONS]
    return pl.pallas_call(
        _attn_mix_kernel,
        grid=(b, s // tm),
        in_specs=specs + specs,
        out_specs=pl.BlockSpec((None, tm, D_ATTN), lambda bi, i: (bi, i, 0)),
        out_shape=jax.ShapeDtypeStruct((b, s, D_ATTN), BF16),
        compiler_params=_params(("parallel", "parallel")),
    )(*os_, *ls_)


def _shift_rows(cur, halo, sh):
    rolled = pltpu.roll(cur, sh, 0)
    hr = pltpu.roll(halo, sh, 0)
    row = lax.broadcasted_iota(jnp.int32, halo.shape, 0)
    first = jnp.where(row < sh, hr, rolled[:SUBLANES])
    return jnp.concatenate([first, rolled[SUBLANES:]], axis=0)


def _causal_conv(cur, halo, w):
    kk = w.shape[0]
    acc = w[kk - 1:kk] * cur
    for sh in range(1, kk):
        acc = acc + w[kk - 1 - sh:kk - sh] * _shift_rows(cur, halo, sh)
    return acc


def _sc_kernel(b_ref, c_ref, h_ref, ch_ref, hh_ref, w_ref, o_ref):
    s = pl.program_id(1)
    g = c_ref[...] * h_ref[...]
    gh = ch_ref[...] * hh_ref[...] * jnp.where(s > 0, 1.0, 0.0)
    o_ref[...] = (b_ref[...] * _causal_conv(g, gh, w_ref[...])).astype(o_ref.dtype)


def _short_conv(sc, conv_w):
    b, s, _ = sc.shape
    ts = _tile(s, 512)
    per = ts // SUBLANES

    def cur(j):
        return pl.BlockSpec((None, ts, D_SC), lambda bi, si: (bi, si, j))

    def halo(j):
        return pl.BlockSpec((None, SUBLANES, D_SC), lambda bi, si: (bi, jnp.maximum(si * per - 1, 0), j))

    return pl.pallas_call(
        _sc_kernel,
        grid=(b, s // ts),
        in_specs=[cur(0), cur(1), cur(2), halo(1), halo(2),
                  pl.BlockSpec((SC_CONV, D_SC), lambda bi, si: (0, 0))],
        out_specs=pl.BlockSpec((None, ts, D_SC), lambda bi, si: (bi, si, 0)),
        out_shape=jax.ShapeDtypeStruct((b, s, D_SC), BF16),
        compiler_params=_params(("parallel", "arbitrary")),
    )(sc, sc, sc, sc, sc, conv_w)


def _ssd_kernel(xs_ref, z_ref, b_ref, c_ref, dt_ref,
                wx_ref, bx_ref, wb_ref, bb_ref, wc_ref, bc_ref,
                dtb_ref, alog_ref, dsk_ref, nw_ref,
                o_ref,
                h_ref, xp_ref, bp_ref, cp_ref, y_ref, *, tc):
    g = pl.program_id(1)
    s = pl.program_id(2)
    q = SSM_CHUNK
    p = SSM_HEAD_DIM
    pads = ((xs_ref, xp_ref, wx_ref, bx_ref), (b_ref, bp_ref, wb_ref, bb_ref), (c_ref, cp_ref, wc_ref, bc_ref))

    @pl.when(s == 0)
    def _():
        h_ref[...] = jnp.zeros_like(h_ref)
        for _, pad_ref, _, _ in pads:
            pad_ref[:SUBLANES, :] = jnp.zeros((SUBLANES, pad_ref.shape[1]), F32)

    for raw_ref, pad_ref, _, _ in pads:
        pad_ref[SUBLANES:, :] = raw_ref[...]

    dt_shift = (LANES - g * SSM_HPG) % LANES
    a_row = -jnp.exp(alog_ref[...])
    dskip = dsk_ref[...]
    li = lax.broadcasted_iota(jnp.int32, (q, q), 0)
    si = lax.broadcasted_iota(jnp.int32, (q, q), 1)
    causal = li >= si
    tri = jnp.where(causal, 1.0, 0.0).astype(F32)
    nt = (((1,), (1,)), ((), ()))

    def conv_silu(pad_ref, w_ref, bias_ref, r0):
        w = w_ref[...]
        win = pad_ref[pl.ds(r0, q + SUBLANES), :]
        acc = bias_ref[...]
        for k in range(SSM_CONV):
            lo = SUBLANES - (SSM_CONV - 1) + k
            acc = acc + w[k:k + 1] * win[lo:lo + q]
        return acc * jax.nn.sigmoid(acc)

    def chunk(ci, carry):
        r0 = pl.multiple_of(ci * q, q)
        xq = conv_silu(xp_ref, wx_ref, bx_ref, r0)
        bq = conv_silu(bp_ref, wb_ref, bb_ref, r0)
        cq = conv_silu(cp_ref, wc_ref, bc_ref, r0)
        dt_raw = pltpu.roll(dt_ref[pl.ds(r0, q), :], dt_shift, 1)
        dtq = jax.nn.softplus(dt_raw + dtb_ref[...])
        cs = jnp.dot(tri, dtq * a_row, precision=lax.Precision.HIGHEST,
                     preferred_element_type=F32)
        cs_t = cs.T
        dt_t = dtq.T
        bb = bq.astype(BF16)
        cb16 = cq.astype(BF16)
        cb = lax.dot_general(cb16, bb, nt, preferred_element_type=F32)
        b_t = bq.T
        hprev = h_ref[...]
        y_off = jnp.dot(cb16, hprev.astype(BF16), preferred_element_type=F32)
        for j in range(SSM_HPG):
            cols = slice(j * p, (j + 1) * p)
            col = cs[:, j:j + 1]
            row = cs_t[j:j + 1, :]
            dtrow = dt_t[j:j + 1, :]
            decay = jnp.exp(jnp.where(causal, col - row, -jnp.inf))
            m = (cb * decay * dtrow).astype(BF16)
            xj = xq[:, cols]
            xj16 = xj.astype(BF16)
            y_d = jnp.dot(m, xj16, preferred_element_type=F32)
            yj = y_d + y_off[:, cols] * jnp.exp(col) + dskip[:, cols] * xj
            y_ref[:, cols] = yj
            last = cs[q - 1:q, j:j + 1]
            w_row = jnp.exp(last - row) * dtrow
            st = jnp.dot((b_t * w_row).astype(BF16), xj16, preferred_element_type=F32)
            h_ref[:, cols] = hprev[:, cols] * jnp.exp(last) + st

        z = z_ref[pl.ds(r0, q), :]
        yv = y_ref[...] * (z * jax.nn.sigmoid(z))
        ms = jnp.mean(jnp.square(yv), axis=-1, keepdims=True)
        o_ref[pl.ds(r0, q), :] = (yv * lax.rsqrt(ms + RMS_EPS) * nw_ref[...]).astype(o_ref.dtype)
        return carry

    lax.fori_loop(0, tc // q, chunk, 0)

    for _, pad_ref, _, _ in pads:
        pad_ref[:SUBLANES, :] = pad_ref[tc:, :]


def _ssd(zx, dt, conv_w, conv_b, dt_bias, a_log, d_skip, norm_w):
    b, s, _ = zx.shape
    tc = _tile(s, 1024)
    gw, ns, hpg = SSM_GW, SSM_STATE, SSM_HPG

    def tok(width, base):
        return pl.BlockSpec((None, tc, width), lambda bi, g, si: (bi, si, base // width + g))

    def par(rows, width):
        return pl.BlockSpec((rows, width), lambda bi, g, si: (0, g))

    def grp(arr):
        a = arr.reshape(SSM_GROUPS, 1, hpg).astype(F32)
        return jnp.pad(a, ((0, 0), (0, 0), (0, LANES - hpg)))

    grp_spec = pl.BlockSpec((None, 1, LANES), lambda bi, g, si: (g, 0, 0))
    wx, wb, wc = conv_w[:, :D_SSM], conv_w[:, D_SSM:D_SSM + 4 * ns], conv_w[:, D_SSM + 4 * ns:]
    cb2 = conv_b.reshape(1, -1)
    bx, bb, bc = cb2[:, :D_SSM], cb2[:, D_SSM:D_SSM + 4 * ns], cb2[:, D_SSM + 4 * ns:]
    d_exp = jnp.repeat(d_skip.astype(F32), SSM_HEAD_DIM).reshape(1, D_SSM)
    return pl.pallas_call(
        functools.partial(_ssd_kernel, tc=tc),
        grid=(b, SSM_GROUPS, s // tc),
        in_specs=[
            tok(gw, ZX_XS), tok(gw, 0), tok(ns, ZX_B), tok(ns, ZX_C),
            pl.BlockSpec((None, tc, LANES), lambda bi, g, si: (bi, si, 0)),
            par(SSM_CONV, gw), par(1, gw),
            par(SSM_CONV, ns), par(1, ns),
            par(SSM_CONV, ns), par(1, ns),
            grp_spec, grp_spec,
            par(1, gw), par(1, gw),
        ],
        out_specs=pl.BlockSpec((None, tc, gw), lambda bi, g, si: (bi, si, g)),
        out_shape=jax.ShapeDtypeStruct((b, s, D_SSM), BF16),
        scratch_shapes=[
            pltpu.VMEM((ns, gw), F32),
            pltpu.VMEM((SUBLANES + tc, gw), F32),
            pltpu.VMEM((SUBLANES + tc, ns), F32),
            pltpu.VMEM((SUBLANES + tc, ns), F32),
            pltpu.VMEM((SSM_CHUNK, gw), F32),
        ],
        compiler_params=_params(("parallel", "parallel", "arbitrary")),
    )(zx, zx, zx, zx, dt,
      wx, bx, wb, bb, wc, bc,
      grp(dt_bias), grp(a_log), d_exp, norm_w.reshape(1, D_SSM).astype(F32))


def _rope_tables(s_len):
    pos = jnp.arange(s_len, dtype=jnp.int32)
    inv_freq = ROPE_THETA ** (-jnp.arange(ROPE_HALF, dtype=F32) / ROPE_HALF)
    ang = pos.astype(F32)[:, None] * inv_freq[None, :]
    cos, sin = jnp.cos(ang), jnp.sin(ang)
    rest = jnp.zeros((s_len, HEAD_DIM - 2 * ROPE_HALF), F32)
    z16 = jnp.zeros((s_len, ROPE_HALF), F32)
    c = jnp.concatenate([cos, cos, jnp.ones_like(rest)], axis=1)
    s1 = jnp.concatenate([-sin, z16, rest], axis=1)
    s2 = jnp.concatenate([z16, sin, rest], axis=1)
    ident = (jnp.ones_like(c), jnp.zeros_like(c), jnp.zeros_like(c))
    return tuple(jnp.stack([t, i]) for t, i in zip((c, s1, s2), ident))


def _stream_major_rows(tab, dil):
    k, s, w = tab.shape
    return tab.reshape(k, s // dil, dil, w).transpose(0, 2, 1, 3).reshape(k, s, w)


def kernel(x, p, w_in, ssm_conv_w, ssm_conv_b, ssm_dt_bias, ssm_a_log, ssm_d, ssm_norm_w, sc_conv_w,
           w_out, ln1_g, ln1_b, w_up, w_down, ln2_g, ln2_b, w_pe, w_gate, ln3_g, ln3_b):
    bsz, s_len, d_model = x.shape
    depth = w_in.shape[0]
    t = bsz * s_len
    d_ple = p.shape[-1]
    alpha = (2.0 * depth) ** 0.25
    tm = _tile(t, MM_TM)
    assert s_len % tm == 0
    rope_blocks = s_len // tm
    rope = _rope_tables(s_len)
    rope_specs = [pl.BlockSpec((None, tm, HEAD_DIM), lambda j, i: (j // 2, i % rope_blocks, 0))] * 3
    rope_by_dil = {d: tuple(_stream_major_rows(tab, d) for tab in rope) for d in ATTN_DILATIONS}
    n_grp = len(ATTN_DILATIONS)
    w_in_t = jnp.swapaxes(w_in, 1, 2)
    w_down16 = w_down.astype(BF16)

    def flat(a):
        return a.reshape(t, a.shape[-1])

    xf = x
    xb, xb4, xb16 = _layer_norm(x, ln1_g[0], ln1_b[0], normalize=False, emit_f32=False, emit_streams=True)
    for i in range(depth):
        streams = {1: xb, 4: xb4, 16: xb16}
        outs = []
        for g, dil in enumerate(ATTN_DILATIONS):
            qkv = _mmw([flat(streams[dil])], [(w_in_t, d_model, 0, lambda j, g=g: j * n_grp + g)], i,
                       3 * D_AGRP, BF16, mode="rope", extra_inputs=rope_by_dil[dil], extra_specs=rope_specs,
                       w_t=True)
            outs.append(_attn_group(qkv.reshape(bsz, dil, s_len // dil, 3 * D_AGRP)))
        y_attn = _attn_mix([o for o, _ in outs], [l for _, l in outs])

        xbf = flat(xb)
        zx = _mmw([xbf], [(w_in_t, d_model, 0, lambda j: U_Z // MM_TN + j)], i, D_ZX, F32, w_t=True)
        dt = _mmw([xbf], [(w_in_t, d_model, 0, lambda j: U_DT // LANES)], i, LANES, F32, tn=LANES, w_t=True)
        sc = _mmw([xbf], [(w_in_t, d_model, 0, lambda j: U_DT // MM_TN + j)], i, 3 * D_SC, F32,
                  w_shift=U_SC - U_DT, w_t=True)
        y_ssm = _ssd(zx.reshape(bsz, s_len, D_ZX), dt.reshape(bsz, s_len, LANES), ssm_conv_w[i], ssm_conv_b[i],
                     ssm_dt_bias[i], ssm_a_log[i], ssm_d[i], ssm_norm_w[i])
        y_sc = _short_conv(sc.reshape(bsz, s_len, 3 * D_SC), sc_conv_w[i])

        resid_spec = pl.BlockSpec((tm, MM_TN), lambda j, i_: (i_, j))
        y1 = _mmw([flat(y_attn), flat(y_ssm), flat(y_sc)],
                  [(w_out, D_ATTN, 0, lambda j: j), (w_out, D_SSM, 1, lambda j: j),
                   (w_out, D_SC, (D_ATTN + D_SSM) // D_SC, lambda j: j)], i,
                  d_model, F32, mode="resid", extra_inputs=(flat(xf),), extra_specs=(resid_spec,), alpha=alpha)
        xf, xb = _layer_norm(y1.reshape(bsz, s_len, d_model), ln1_g[i], ln1_b[i])

        hid = _mmw([flat(xb)], [(w_up, d_model, 0, lambda j: j)], i, w_up.shape[2], BF16, mode="relu2")
        y2 = _mmk_resid(hid, w_down16, i, flat(xf), alpha)
        xf, xb = _layer_norm(y2.reshape(bsz, s_len, d_model), ln2_g[i], ln2_b[i])

        y3 = _mmw([flat(xb)], [(w_gate, d_model, 0, lambda j: j)], i, d_model, F32, mode="gate",
                  extra_inputs=(flat(xf), flat(p[i]), w_pe[i]),
                  extra_specs=(resid_spec, pl.BlockSpec((tm, d_ple), lambda j, i_: (i_, 0)),
                               pl.BlockSpec((d_ple, MM_TN), lambda j, i_: (0, j))),
                  alpha=alpha)
        last = i == depth - 1
        res = _layer_norm(y3.reshape(bsz, s_len, d_model), ln3_g[i], ln3_b[i],
                          emit_bf16=not last, emit_streams=not last)
        if last:
            xf, = res
        else:
            xf, xb, xb4, xb16 = res
    return xf
```

```python
import functools

import jax
import jax.numpy as jnp
from jax import lax
from jax.experimental import pallas as pl
from jax.experimental.pallas import tpu as pltpu

HEAD_DIM = 128
ATTN_DILATIONS = (1, 4, 16)
ATTN_HPG = 4
ATTN_BLOCK = 128
ATTN_ROWS = 32
D_ATTN = ATTN_HPG * len(ATTN_DILATIONS) * HEAD_DIM
D_AGRP = ATTN_HPG * HEAD_DIM
ROPE_THETA = 500000.0
ROPE_HALF = HEAD_DIM // 8
D_SSM = 1536
SSM_HEAD_DIM = 64
SSM_HEADS = D_SSM // SSM_HEAD_DIM
SSM_GROUPS = 4
SSM_HPG = SSM_HEADS // SSM_GROUPS
SSM_GW = D_SSM // SSM_GROUPS
SSM_STATE = 128
SSM_CONV = 4
SSM_CHUNK = 128
D_SC = 1024
SC_CONV = 3
LN_EPS = 1e-5
RMS_EPS = 1e-5

U_Z = 3 * D_ATTN
U_DT = U_Z + 2 * D_SSM + 2 * SSM_GROUPS * SSM_STATE
U_SC = U_DT + SSM_HEADS
D_ZX = U_DT - U_Z
ZX_XS = D_SSM
ZX_B = 2 * D_SSM
ZX_C = ZX_B + SSM_GROUPS * SSM_STATE

LANES = 128
SUBLANES = 8
VMEM_LIMIT_BYTES = 56 * 1024 * 1024

BF16 = jnp.bfloat16
F32 = jnp.float32

MM_TM = 1024
MM_TN = 512
LN_TM = 256
LN_ROW_CHUNK = 32


def _params(sem):
    return pltpu.CompilerParams(dimension_semantics=sem, vmem_limit_bytes=VMEM_LIMIT_BYTES)


def _tile(n, pref):
    t = min(n, pref)
    while n % t:
        t -= 1
    return t


def _log2(n):
    assert n > 0 and n & (n - 1) == 0, n
    return n.bit_length() - 1


def _mmw_kernel(*refs, n_x, mode, alpha, w_shift, w_t):
    xs, ws = refs[:n_x], refs[n_x:2 * n_x]
    pos = 2 * n_x
    if w_shift:
        w_next_ref = refs[pos]
        pos += 1
    if mode == "rope":
        c_ref, s1_ref, s2_ref = refs[pos:pos + 3]
        pos += 3
    elif mode == "resid":
        r_ref = refs[pos]
        pos += 1
    elif mode == "gate":
        r_ref, p_ref, wpe_ref = refs[pos:pos + 3]
        pos += 3
    o_ref = refs[pos]
    wbs = refs[pos + 1:pos + 1 + n_x]

    @pl.when(pl.program_id(1) == 0)
    def _():
        if w_shift:
            keep = wbs[0].shape[0] - w_shift
            wbs[0][:keep, :] = ws[0][w_shift:, :].astype(BF16)
            wbs[0][keep:, :] = w_next_ref[:w_shift, :].astype(BF16)
        else:
            for w_ref, wb_ref in zip(ws, wbs):
                wb_ref[...] = w_ref[...].astype(BF16)

    contract = (((1,), (1,)), ((), ())) if w_t else (((1,), (0,)), ((), ()))
    tm, tn = o_ref.shape
    halves = {"rope": 4, "gate": 2}.get(mode, 1)
    halves = halves if tm % (halves * 2 * SUBLANES) == 0 else 1
    rows_per = tm // halves
    for hf in range(halves):
        rs = slice(hf * rows_per, (hf + 1) * rows_per)
        acc = None
        for x_ref, wb_ref in zip(xs, wbs):
            part = lax.dot_general(x_ref[rs, :], wb_ref[...], contract, preferred_element_type=F32)
            acc = part if acc is None else acc + part

        if mode == "rope":
            rep = tn // HEAD_DIM
            acc = (acc * jnp.tile(c_ref[rs, :], (1, rep))
                   + pltpu.roll(acc, tn - ROPE_HALF, 1) * jnp.tile(s1_ref[rs, :], (1, rep))
                   + pltpu.roll(acc, ROPE_HALF, 1) * jnp.tile(s2_ref[rs, :], (1, rep)))
        elif mode == "relu2":
            acc = jnp.square(jnp.maximum(acc, 0.0))
        elif mode == "resid":
            acc = alpha * r_ref[rs, :] + acc
        elif mode == "gate":
            emb = jnp.dot(p_ref[rs, :].astype(BF16), wpe_ref[...].astype(BF16), preferred_element_type=F32)
            acc = alpha * r_ref[rs, :] + jax.nn.sigmoid(acc) * emb
        o_ref[rs, :] = acc.astype(o_ref.dtype)


def _mmw(xs, w_specs, layer, n, out_dtype, mode="plain", extra_inputs=(), extra_specs=(), alpha=1.0, tn=MM_TN,
         w_shift=0, w_t=False):
    t = xs[0].shape[0]
    tm = _tile(t, MM_TM)
    assert n % tn == 0
    in_specs = [pl.BlockSpec((tm, x.shape[1]), lambda j, i: (i, 0)) for x in xs]
    w_arrays = [w for w, _, _, _ in w_specs]

    def w_spec(kdim, kblk, cfn, off=0, **kw):
        if w_t:
            return pl.BlockSpec((None, tn, kdim), lambda j, i: (layer, cfn(j) + off, kblk), **kw)
        return pl.BlockSpec((None, kdim, tn), lambda j, i: (layer, kblk, cfn(j) + off), **kw)

    for _, kdim, kblk, cfn in w_specs:
        in_specs.append(w_spec(kdim, kblk, cfn))
    if w_shift:
        assert w_t and len(w_specs) == 1 and w_shift % SUBLANES == 0
        w, kdim, kblk, cfn = w_specs[0]
        in_specs.append(w_spec(kdim, kblk, cfn, off=1, pipeline_mode=pl.Buffered(1)))
        w_arrays.append(w)
    return pl.pallas_call(
        functools.partial(_mmw_kernel, n_x=len(xs), mode=mode, alpha=alpha, w_shift=w_shift, w_t=w_t),
        grid=(n // tn, t // tm),
        in_specs=in_specs + list(extra_specs),
        out_specs=pl.BlockSpec((tm, tn), lambda j, i: (i, j)),
        out_shape=jax.ShapeDtypeStruct((t, n), out_dtype),
        scratch_shapes=[pltpu.VMEM((tn, kdim) if w_t else (kdim, tn), BF16) for _, kdim, _, _ in w_specs],
        compiler_params=_params(("arbitrary", "arbitrary")),
    )(*xs, *w_arrays, *extra_inputs)


def _mmk_kernel(x_ref, w_ref, r_ref, o_ref, acc_ref, *, alpha):
    k = pl.program_id(2)

    @pl.when(k == 0)
    def _():
        acc_ref[...] = jnp.zeros_like(acc_ref)

    acc_ref[...] += jnp.dot(x_ref[...], w_ref[...], preferred_element_type=F32)

    @pl.when(k == pl.num_programs(2) - 1)
    def _():
        o_ref[...] = alpha * r_ref[...] + acc_ref[...]


def _mmk_resid(x, w, layer, resid, alpha):
    t, kdim = x.shape
    n = w.shape[2]
    tm, tn, tk = _tile(t, 1024), _tile(n, 1024), _tile(kdim, 2048)
    return pl.pallas_call(
        functools.partial(_mmk_kernel, alpha=alpha),
        grid=(t // tm, n // tn, kdim // tk),
        in_specs=[pl.BlockSpec((tm, tk), lambda i, j, k: (i, k)),
                  pl.BlockSpec((None, tk, tn), lambda i, j, k: (layer, k, j)),
                  pl.BlockSpec((tm, tn), lambda i, j, k: (i, j))],
        out_specs=pl.BlockSpec((tm, tn), lambda i, j, k: (i, j)),
        out_shape=jax.ShapeDtypeStruct((t, n), F32),
        scratch_shapes=[pltpu.VMEM((tm, tn), F32)],
        compiler_params=_params(("parallel", "parallel", "arbitrary")),
    )(x, w, resid)


def _stream_perm(tm, dil, inverse):
    n = tm // dil
    a = lax.broadcasted_iota(jnp.int32, (tm, tm), 0)
    b = lax.broadcasted_iota(jnp.int32, (tm, tm), 1)
    if inverse:
        src = jnp.bitwise_and(a, dil - 1) * n + lax.shift_right_logical(a, _log2(dil))
    else:
        src = jnp.bitwise_and(a, n - 1) * dil + lax.shift_right_logical(a, _log2(n))
    return b == src


def _ln_kernel(*refs, normalize, emit_f32, emit_bf16, emit_streams):
    y_ref, g_ref, b_ref = refs[:3]
    outs = list(refs[3:])
    f32_ref = outs.pop(0) if emit_f32 else None
    bf16_ref = outs.pop(0) if emit_bf16 else None
    assert emit_bf16 or not emit_streams
    tm = y_ref.shape[0]
    rows = _tile(tm, LN_ROW_CHUNK)

    def norm_rows(ri, carry):
        rs = pl.ds(pl.multiple_of(ri * rows, rows), rows)
        y = y_ref[rs, :]
        if normalize:
            mu = jnp.mean(y, axis=-1, keepdims=True)
            yc = y - mu
            var = jnp.mean(jnp.square(yc), axis=-1, keepdims=True)
            y = yc * lax.rsqrt(var + LN_EPS) * g_ref[...] + b_ref[...]
        if emit_f32:
            f32_ref[rs, :] = y
        if emit_bf16:
            bf16_ref[rs, :] = y.astype(BF16)
        return carry

    lax.fori_loop(0, tm // rows, norm_rows, 0)

    if emit_streams:
        y16 = bf16_ref[...]
        for dil in ATTN_DILATIONS[1:]:
            o_ref = outs.pop(0)
            perm = jnp.where(_stream_perm(tm, dil, inverse=False), 1.0, 0.0).astype(BF16)
            moved = jnp.dot(perm, y16, preferred_element_type=F32).astype(BF16)
            n = tm // dil
            for r in range(dil):
                o_ref[r] = moved[r * n:(r + 1) * n]


def _layer_norm(y3, gain, bias, normalize=True, emit_f32=True, emit_bf16=True, emit_streams=False):
    b, s, d = y3.shape
    tm = _tile(s, LN_TM if emit_streams else 2 * LN_TM)
    tok = pl.BlockSpec((None, tm, d), lambda bi, i: (bi, i, 0))
    vec = pl.BlockSpec((1, d), lambda bi, i: (0, 0))
    out_specs, out_shape = [], []
    if emit_f32:
        out_specs.append(tok)
        out_shape.append(jax.ShapeDtypeStruct((b, s, d), F32))
    if emit_bf16:
        out_specs.append(tok)
        out_shape.append(jax.ShapeDtypeStruct((b, s, d), BF16))
    if emit_streams:
        for dil in ATTN_DILATIONS[1:]:
            out_specs.append(pl.BlockSpec((None, dil, tm // dil, d), lambda bi, i: (bi, 0, i, 0)))
            out_shape.append(jax.ShapeDtypeStruct((b, dil, s // dil, d), BF16))
    return pl.pallas_call(
        functools.partial(_ln_kernel, normalize=normalize, emit_f32=emit_f32, emit_bf16=emit_bf16,
                          emit_streams=emit_streams),
        grid=(b, s // tm),
        in_specs=[tok, vec, vec],
        out_specs=out_specs,
        out_shape=out_shape,
        compiler_params=_params(("parallel", "parallel")),
    )(y3, gain.reshape(1, d), bias.reshape(1, d))


def _attn_kernel(q_ref, kc_ref, kp_ref, vc_ref, vp_ref, o_ref, l_ref, s_scr, p_scr, m_scr, bias_scr, *, tq, scale):
    n = pl.program_id(2)
    blk = ATTN_BLOCK
    units = [(h, i) for h in range(ATTN_HPG) for i in range(tq // blk)]
    nt = (((1,), (1,)), ((), ()))

    def keys_vals(ref_c, ref_p, h, i):
        hs = slice(h * HEAD_DIM, (h + 1) * HEAD_DIM)
        if i == 0:
            return jnp.concatenate([ref_p[:, hs], ref_c[:blk, hs]], axis=0)
        return ref_c[(i - 1) * blk:(i + 1) * blk, hs]

    row = lax.broadcasted_iota(jnp.int32, (blk, 2 * blk), 0)
    col = lax.broadcasted_iota(jnp.int32, (blk, 2 * blk), 1)
    bias_scr[...] = jnp.where(col < blk,
                              jnp.where(col >= row, 0.0, -jnp.inf),
                              jnp.where(col - blk <= row, 0.0, -jnp.inf))
    col_r = lax.broadcasted_iota(jnp.int32, (ATTN_ROWS, 2 * blk), 1)
    first_pen = jnp.where(col_r < blk, jnp.where(n == 0, -jnp.inf, 0.0), 0.0)

    for u, (h, i) in enumerate(units):
        q = q_ref[i * blk:(i + 1) * blk, h * HEAD_DIM:(h + 1) * HEAD_DIM]
        s_scr[u] = lax.dot_general(q, keys_vals(kc_ref, kp_ref, h, i), nt, preferred_element_type=F32)

    for u, (h, i) in enumerate(units):
        for r in range(blk // ATTN_ROWS):
            rs = slice(r * ATTN_ROWS, (r + 1) * ATTN_ROWS)
            s = s_scr[u, rs, :] * scale + bias_scr[rs, :]
            if i == 0:
                s = s + first_pen
            m = jnp.max(s, axis=-1, keepdims=True)
            p_scr[u, rs, :] = jnp.exp(s - m).astype(BF16)
            m_scr[u, rs, :] = jnp.broadcast_to(m, (ATTN_ROWS, HEAD_DIM))

    ones = jnp.ones((2 * blk, HEAD_DIM), BF16)
    for u, (h, i) in enumerate(units):
        v_aug = jnp.concatenate([keys_vals(vc_ref, vp_ref, h, i), ones], axis=1)
        pv = jnp.dot(p_scr[u], v_aug, preferred_element_type=F32)
        den = pv[:, HEAD_DIM:]
        rs, hs = slice(i * blk, (i + 1) * blk), slice(h * HEAD_DIM, (h + 1) * HEAD_DIM)
        o_ref[rs, hs] = pv[:, :HEAD_DIM] / den
        l_ref[rs, hs] = m_scr[u] + jnp.log(den)


def _attn_group(qkv):
    b, dil, sub, _ = qkv.shape
    tq = _tile(sub, 512)
    per = tq // ATTN_BLOCK
    n_units = ATTN_HPG * per

    def cur(c):
        return pl.BlockSpec((None, None, tq, D_AGRP), lambda bi, r, n: (bi, r, n, c))

    def prev(c):
        return pl.BlockSpec((None, None, ATTN_BLOCK, D_AGRP),
                            lambda bi, r, n: (bi, r, jnp.maximum(n * per - 1, 0), c))

    out_sds = jax.ShapeDtypeStruct((b, dil, sub, D_AGRP), F32)
    return pl.pallas_call(
        functools.partial(_attn_kernel, tq=tq, scale=HEAD_DIM ** -0.5),
        grid=(b, dil, sub // tq),
        in_specs=[cur(0), cur(1), prev(1), cur(2), prev(2)],
        out_specs=[cur(0), cur(0)],
        out_shape=[out_sds, out_sds],
        scratch_shapes=[pltpu.VMEM((n_units, ATTN_BLOCK, 2 * ATTN_BLOCK), F32),
                        pltpu.VMEM((n_units, ATTN_BLOCK, 2 * ATTN_BLOCK), BF16),
                        pltpu.VMEM((n_units, ATTN_BLOCK, HEAD_DIM), F32),
                        pltpu.VMEM((ATTN_BLOCK, 2 * ATTN_BLOCK), F32)],
        compiler_params=_params(("parallel", "parallel", "arbitrary")),
    )(qkv, qkv, qkv, qkv, qkv)


def _to_token_order(x_ref, dil):
    if dil == 1:
        return x_ref[0]
    n, w = x_ref.shape[1], x_ref.shape[2]
    x = x_ref[...].reshape(dil * n, w)
    perm = jnp.where(_stream_perm(dil * n, dil, inverse=True), 1.0, 0.0).astype(BF16)
    hi = x.astype(BF16)
    r1 = x - hi.astype(F32)
    mid = r1.astype(BF16)
    lo = (r1 - mid.astype(F32)).astype(BF16)
    return (jnp.dot(perm, hi, preferred_element_type=F32)
            + jnp.dot(perm, mid, preferred_element_type=F32)
            + jnp.dot(perm, lo, preferred_element_type=F32))


def _attn_mix_kernel(o0, o1, o2, l0, l1, l2, y_ref):
    os_ = [_to_token_order(o, d) for o, d in zip((o0, o1, o2), ATTN_DILATIONS)]
    ls = [_to_token_order(l, d) for l, d in zip((l0, l1, l2), ATTN_DILATIONS)]
    m = jnp.maximum(jnp.maximum(ls[0], ls[1]), ls[2])
    es = [jnp.exp(l - m) for l in ls]
    inv = 1.0 / (es[0] + es[1] + es[2])
    for g in range(len(ATTN_DILATIONS)):
        y_ref[:, g * D_AGRP:(g + 1) * D_AGRP] = (os_[g] * (es[g] * inv)).astype(y_ref.dtype)


def _attn_mix(os_, ls_):
    b, _, s, _ = os_[0].shape
    tm = _tile(s, LN_TM)
    specs = [pl.BlockSpec((None, d, tm // d, D_AGRP), lambda bi, i: (bi, 0, i, 0)) for d in ATTN_DILATIONS]
    return pl.pallas_call(
        _attn_mix_kernel,
        grid=(b, s // tm),
        in_specs=specs + specs,
        out_specs=pl.BlockSpec((None, tm, D_ATTN), lambda bi, i: (bi, i, 0)),
        out_shape=jax.ShapeDtypeStruct((b, s, D_ATTN), BF16),
        compiler_params=_params(("parallel", "parallel")),
    )(*os_, *ls_)


def _shift_rows(cur, halo, sh):
    rolled = pltpu.roll(cur, sh, 0)
    hr = pltpu.roll(halo, sh, 0)
    row = lax.broadcasted_iota(jnp.int32, halo.shape, 0)
    first = jnp.where(row < sh, hr, rolled[:SUBLANES])
    return jnp.concatenate([first, rolled[SUBLANES:]], axis=0)


def _causal_conv(cur, halo, w):
    kk = w.shape[0]
    acc = w[kk - 1:kk] * cur
    for sh in range(1, kk):
        acc = acc + w[kk - 1 - sh:kk - sh] * _shift_rows(cur, halo, sh)
    return acc


def _sc_kernel(b_ref, c_ref, h_ref, ch_ref, hh_ref, w_ref, o_ref):
    s = pl.program_id(1)
    g = c_ref[...] * h_ref[...]
    gh = ch_ref[...] * hh_ref[...] * jnp.where(s > 0, 1.0, 0.0)
    o_ref[...] = (b_ref[...] * _causal_conv(g, gh, w_ref[...])).astype(o_ref.dtype)


def _short_conv(sc, conv_w):
    b, s, _ = sc.shape
    ts = _tile(s, 512)
    per = ts // SUBLANES

    def cur(j):
        return pl.BlockSpec((None, ts, D_SC), lambda bi, si: (bi, si, j))

    def halo(j):
        return pl.BlockSpec((None, SUBLANES, D_SC), lambda bi, si: (bi, jnp.maximum(si * per - 1, 0), j))

    return pl.pallas_call(
        _sc_kernel,
        grid=(b, s // ts),
        in_specs=[cur(0), cur(1), cur(2), halo(1), halo(2),
                  pl.BlockSpec((SC_CONV, D_SC), lambda bi, si: (0, 0))],
        out_specs=pl.BlockSpec((None, ts, D_SC), lambda bi, si: (bi, si, 0)),
        out_shape=jax.ShapeDtypeStruct((b, s, D_SC), BF16),
        compiler_params=_params(("parallel", "arbitrary")),
    )(sc, sc, sc, sc, sc, conv_w)


def _ssd_kernel(xs_ref, z_ref, b_ref, c_ref, dt_ref,
                wx_ref, bx_ref, wb_ref, bb_ref, wc_ref, bc_ref,
                dtb_ref, alog_ref, dsk_ref, nw_ref,
                o_ref,
                h_ref, xp_ref, bp_ref, cp_ref, y_ref, *, tc):
    g = pl.program_id(1)
    s = pl.program_id(2)
    q = SSM_CHUNK
    p = SSM_HEAD_DIM
    pads = ((xs_ref, xp_ref, wx_ref, bx_ref), (b_ref, bp_ref, wb_ref, bb_ref), (c_ref, cp_ref, wc_ref, bc_ref))

    @pl.when(s == 0)
    def _():
        h_ref[...] = jnp.zeros_like(h_ref)
        for _, pad_ref, _, _ in pads:
            pad_ref[:SUBLANES, :] = jnp.zeros((SUBLANES, pad_ref.shape[1]), F32)

    for raw_ref, pad_ref, _, _ in pads:
        pad_ref[SUBLANES:, :] = raw_ref[...]

    dt_shift = (LANES - g * SSM_HPG) % LANES
    a_row = -jnp.exp(alog_ref[...])
    dskip = dsk_ref[...]
    li = lax.broadcasted_iota(jnp.int32, (q, q), 0)
    si = lax.broadcasted_iota(jnp.int32, (q, q), 1)
    causal = li >= si
    tri = jnp.where(causal, 1.0, 0.0).astype(F32)
    first_head = lax.broadcasted_iota(jnp.int32, (q, 2 * p), 1) < p
    nt = (((1,), (1,)), ((), ()))

    def conv_silu(pad_ref, w_ref, bias_ref, r0):
        w = w_ref[...]
        win = pad_ref[pl.ds(r0, q + SUBLANES), :]
        acc = bias_ref[...]
        for k in range(SSM_CONV):
            lo = SUBLANES - (SSM_CONV - 1) + k
            acc = acc + w[k:k + 1] * win[lo:lo + q]
        return acc * jax.nn.sigmoid(acc)

    def chunk(ci, carry):
        r0 = pl.multiple_of(ci * q, q)
        xq = conv_silu(xp_ref, wx_ref, bx_ref, r0)
        bq = conv_silu(bp_ref, wb_ref, bb_ref, r0)
        cq = conv_silu(cp_ref, wc_ref, bc_ref, r0)
        dt_raw = pltpu.roll(dt_ref[pl.ds(r0, q), :], dt_shift, 1)
        dtq = jax.nn.softplus(dt_raw + dtb_ref[...])
        cs = jnp.dot(tri, dtq * a_row, precision=lax.Precision.HIGHEST,
                     preferred_element_type=F32)
        cs_t = cs.T
        dt_t = dtq.T
        bb = bq.astype(BF16)
        cb16 = cq.astype(BF16)
        cb = lax.dot_general(cb16, bb, nt, preferred_element_type=F32)
        b_t = bq.T
        hprev = h_ref[...]
        y_off = jnp.dot(cb16, hprev.astype(BF16), preferred_element_type=F32)
        for jp in range(SSM_HPG // 2):
            cols = slice(jp * 2 * p, (jp + 1) * 2 * p)
            x2 = xq[:, cols]
            y_d, st, e_col, e_last = None, None, [], []
            for half in range(2):
                j = 2 * jp + half
                xh = (jnp.where(first_head, x2, 0.0) if half == 0 else jnp.where(first_head, 0.0, x2)).astype(BF16)
                col = cs[:, j:j + 1]
                row = cs_t[j:j + 1, :]
                dtrow = dt_t[j:j + 1, :]
                decay = jnp.exp(jnp.where(causal, col - row, -jnp.inf))
                m = (cb * decay * dtrow).astype(BF16)
                part = jnp.dot(m, xh, preferred_element_type=F32)
                y_d = part if y_d is None else y_d + part
                last = cs[q - 1:q, j:j + 1]
                w_row = jnp.exp(last - row) * dtrow
                part = jnp.dot((b_t * w_row).astype(BF16), xh, preferred_element_type=F32)
                st = part if st is None else st + part
                e_col.append(jnp.exp(col))
                e_last.append(jnp.exp(last))
            y_ref[:, cols] = (y_d + y_off[:, cols] * jnp.where(first_head, e_col[0], e_col[1])
                              + dskip[:, cols] * x2)
            h_ref[:, cols] = hprev[:, cols] * jnp.where(first_head[:1], e_last[0], e_last[1]) + st

        z = z_ref[pl.ds(r0, q), :]
        yv = y_ref[...] * (z * jax.nn.sigmoid(z))
        ms = jnp.mean(jnp.square(yv), axis=-1, keepdims=True)
        o_ref[pl.ds(r0, q), :] = (yv * lax.rsqrt(ms + RMS_EPS) * nw_ref[...]).astype(o_ref.dtype)
        return carry

    lax.fori_loop(0, tc // q, chunk, 0)

    for _, pad_ref, _, _ in pads:
        pad_ref[:SUBLANES, :] = pad_ref[tc:, :]


def _ssd(zx, dt, conv_w, conv_b, dt_bias, a_log, d_skip, norm_w):
    b, s, _ = zx.shape
    tc = _tile(s, 1024)
    gw, ns, hpg = SSM_GW, SSM_STATE, SSM_HPG

    def tok(width, base):
        return pl.BlockSpec((None, tc, width), lambda bi, g, si: (bi, si, base // width + g))

    def par(rows, width):
        return pl.BlockSpec((rows, width), lambda bi, g, si: (0, g))

    def grp(arr):
        a = arr.reshape(SSM_GROUPS, 1, hpg).astype(F32)
        return jnp.pad(a, ((0, 0), (0, 0), (0, LANES - hpg)))

    grp_spec = pl.BlockSpec((None, 1, LANES), lambda bi, g, si: (g, 0, 0))
    wx, wb, wc = conv_w[:, :D_SSM], conv_w[:, D_SSM:D_SSM + 4 * ns], conv_w[:, D_SSM + 4 * ns:]
    cb2 = conv_b.reshape(1, -1)
    bx, bb, bc = cb2[:, :D_SSM], cb2[:, D_SSM:D_SSM + 4 * ns], cb2[:, D_SSM + 4 * ns:]
    d_exp = jnp.repeat(d_skip.astype(F32), SSM_HEAD_DIM).reshape(1, D_SSM)
    return pl.pallas_call(
        functools.partial(_ssd_kernel, tc=tc),
        grid=(b, SSM_GROUPS, s // tc),
        in_specs=[
            tok(gw, ZX_XS), tok(gw, 0), tok(ns, ZX_B), tok(ns, ZX_C),
            pl.BlockSpec((None, tc, LANES), lambda bi, g, si: (bi, si, 0)),
            par(SSM_CONV, gw), par(1, gw),
            par(SSM_CONV, ns), par(1, ns),
            par(SSM_CONV, ns), par(1, ns),
            grp_spec, grp_spec,
            par(1, gw), par(1, gw),
        ],
        out_specs=pl.BlockSpec((None, tc, gw), lambda bi, g, si: (bi, si, g)),
        out_shape=jax.ShapeDtypeStruct((b, s, D_SSM), BF16),
        scratch_shapes=[
            pltpu.VMEM((ns, gw), F32),
            pltpu.VMEM((SUBLANES + tc, gw), F32),
            pltpu.VMEM((SUBLANES + tc, ns), F32),
            pltpu.VMEM((SUBLANES + tc, ns), F32),
            pltpu.VMEM((SSM_CHUNK, gw), F32),
        ],
        compiler_params=_params(("parallel", "parallel", "arbitrary")),
    )(zx, zx, zx, zx, dt,
      wx, bx, wb, bb, wc, bc,
      grp(dt_bias), grp(a_log), d_exp, norm_w.reshape(1, D_SSM).astype(F32))


def _rope_tables(s_len):
    pos = jnp.arange(s_len, dtype=jnp.int32)
    inv_freq = ROPE_THETA ** (-jnp.arange(ROPE_HALF, dtype=F32) / ROPE_HALF)
    ang = pos.astype(F32)[:, None] * inv_freq[None, :]
    cos, sin = jnp.cos(ang), jnp.sin(ang)
    rest = jnp.zeros((s_len, HEAD_DIM - 2 * ROPE_HALF), F32)
    z16 = jnp.zeros((s_len, ROPE_HALF), F32)
    c = jnp.concatenate([cos, cos, jnp.ones_like(rest)], axis=1)
    s1 = jnp.concatenate([-sin, z16, rest], axis=1)
    s2 = jnp.concatenate([z16, sin, rest], axis=1)
    ident = (jnp.ones_like(c), jnp.zeros_like(c), jnp.zeros_like(c))
    return tuple(jnp.stack([t, i]) for t, i in zip((c, s1, s2), ident))


def _stream_major_rows(tab, dil):
    k, s, w = tab.shape
    return tab.reshape(k, s // dil, dil, w).transpose(0, 2, 1, 3).reshape(k, s, w)


def kernel(x, p, w_in, ssm_conv_w, ssm_conv_b, ssm_dt_bias, ssm_a_log, ssm_d, ssm_norm_w, sc_conv_w,
           w_out, ln1_g, ln1_b, w_up, w_down, ln2_g, ln2_b, w_pe, w_gate, ln3_g, ln3_b):
    bsz, s_len, d_model = x.shape
    depth = w_in.shape[0]
    t = bsz * s_len
    d_ple = p.shape[-1]
    alpha = (2.0 * depth) ** 0.25
    tm = _tile(t, MM_TM)
    assert s_len % tm == 0
    rope_blocks = s_len // tm
    rope = _rope_tables(s_len)
    rope_specs = [pl.BlockSpec((None, tm, HEAD_DIM), lambda j, i: (j // 2, i % rope_blocks, 0))] * 3
    rope_by_dil = {d: tuple(_stream_major_rows(tab, d) for tab in rope) for d in ATTN_DILATIONS}
    n_grp = len(ATTN_DILATIONS)
    w_in_t = jnp.swapaxes(w_in, 1, 2)
    w_down16 = w_down.astype(BF16)

    def flat(a):
        return a.reshape(t, a.shape[-1])

    xf = x
    xb, xb4, xb16 = _layer_norm(x, ln1_g[0], ln1_b[0], normalize=False, emit_f32=False, emit_streams=True)
    for i in range(depth):
        streams = {1: xb, 4: xb4, 16: xb16}
        outs = []
        for g, dil in enumerate(ATTN_DILATIONS):
            qkv = _mmw([flat(streams[dil])], [(w_in_t, d_model, 0, lambda j, g=g: j * n_grp + g)], i,
                       3 * D_AGRP, BF16, mode="rope", extra_inputs=rope_by_dil[dil], extra_specs=rope_specs,
                       w_t=True)
            outs.append(_attn_group(qkv.reshape(bsz, dil, s_len // dil, 3 * D_AGRP)))
        y_attn = _attn_mix([o for o, _ in outs], [l for _, l in outs])

        xbf = flat(xb)
        zx = _mmw([xbf], [(w_in_t, d_model, 0, lambda j: U_Z // MM_TN + j)], i, D_ZX, F32, w_t=True)
        dt = _mmw([xbf], [(w_in_t, d_model, 0, lambda j: U_DT // LANES)], i, LANES, F32, tn=LANES, w_t=True)
        sc = _mmw([xbf], [(w_in_t, d_model, 0, lambda j: U_DT // MM_TN + j)], i, 3 * D_SC, F32,
                  w_shift=U_SC - U_DT, w_t=True)
        y_ssm = _ssd(zx.reshape(bsz, s_len, D_ZX), dt.reshape(bsz, s_len, LANES), ssm_conv_w[i], ssm_conv_b[i],
                     ssm_dt_bias[i], ssm_a_log[i], ssm_d[i], ssm_norm_w[i])
        y_sc = _short_conv(sc.reshape(bsz, s_len, 3 * D_SC), sc_conv_w[i])

        resid_spec = pl.BlockSpec((tm, MM_TN), lambda j, i_: (i_, j))
        y1 = _mmw([flat(y_attn), flat(y_ssm), flat(y_sc)],
                  [(w_out, D_ATTN, 0, lambda j: j), (w_out, D_SSM, 1, lambda j: j),
                   (w_out, D_SC, (D_ATTN + D_SSM) // D_SC, lambda j: j)], i,
                  d_model, F32, mode="resid", extra_inputs=(flat(xf),), extra_specs=(resid_spec,), alpha=alpha)
        xf, xb = _layer_norm(y1.reshape(bsz, s_len, d_model), ln1_g[i], ln1_b[i])

        hid = _mmw([flat(xb)], [(w_up, d_model, 0, lambda j: j)], i, w_up.shape[2], BF16, mode="relu2")
        y2 = _mmk_resid(hid, w_down16, i, flat(xf), alpha)
        xf, xb = _layer_norm(y2.reshape(bsz, s_len, d_model), ln2_g[i], ln2_b[i])

        y3 = _mmw([flat(xb)], [(w_gate, d_model, 0, lambda j: j)], i, d_model, F32, mode="gate",
                  extra_inputs=(flat(xf), flat(p[i]), w_pe[i]),
                  extra_specs=(resid_spec, pl.BlockSpec((tm, d_ple), lambda j, i_: (i_, 0)),
                               pl.BlockSpec((d_ple, MM_TN), lambda j, i_: (0, j))),
                  alpha=alpha)
        last = i == depth - 1
        res = _layer_norm(y3.reshape(bsz, s_len, d_model), ln3_g[i], ln3_b[i],
                          emit_bf16=not last, emit_streams=not last)
        if last:
            xf, = res
        else:
            xf, xb, xb4, xb16 = res
    return xf
```

```python
import functools

import jax
import jax.numpy as jnp
from jax import lax
from jax.experimental import pallas as pl
from jax.experimental.pallas import tpu as pltpu

HEAD_DIM = 128
ATTN_DILATIONS = (1, 4, 16)
ATTN_HPG = 4
ATTN_BLOCK = 128
ATTN_ROWS = 32
D_ATTN = ATTN_HPG * len(ATTN_DILATIONS) * HEAD_DIM
D_AGRP = ATTN_HPG * HEAD_DIM
ROPE_THETA = 500000.0
ROPE_HALF = HEAD_DIM // 8
D_SSM = 1536
SSM_HEAD_DIM = 64
SSM_HEADS = D_SSM // SSM_HEAD_DIM
SSM_GROUPS = 4
SSM_HPG = SSM_HEADS // SSM_GROUPS
SSM_GW = D_SSM // SSM_GROUPS
SSM_STATE = 128
SSM_CONV = 4
SSM_CHUNK = 128
D_SC = 1024
SC_CONV = 3
LN_EPS = 1e-5
RMS_EPS = 1e-5

U_Z = 3 * D_ATTN
U_DT = U_Z + 2 * D_SSM + 2 * SSM_GROUPS * SSM_STATE
U_SC = U_DT + SSM_HEADS
D_ZX = U_DT - U_Z
ZX_XS = D_SSM
ZX_B = 2 * D_SSM
ZX_C = ZX_B + SSM_GROUPS * SSM_STATE

LANES = 128
SUBLANES = 8
VMEM_LIMIT_BYTES = 56 * 1024 * 1024

BF16 = jnp.bfloat16
F32 = jnp.float32

MM_TM = 1024
MM_TN = 512
LN_TM = 256
LN_ROW_CHUNK = 32


def _params(sem):
    return pltpu.CompilerParams(dimension_semantics=sem, vmem_limit_bytes=VMEM_LIMIT_BYTES)


def _tile(n, pref):
    t = min(n, pref)
    while n % t:
        t -= 1
    return t


def _log2(n):
    assert n > 0 and n & (n - 1) == 0, n
    return n.bit_length() - 1


def _mmw_kernel(*refs, n_x, mode, alpha, w_shift, w_t):
    xs, ws = refs[:n_x], refs[n_x:2 * n_x]
    pos = 2 * n_x
    if w_shift:
        w_next_ref = refs[pos]
        pos += 1
    if mode == "rope":
        c_ref, s1_ref, s2_ref = refs[pos:pos + 3]
        pos += 3
    elif mode == "resid":
        r_ref = refs[pos]
        pos += 1
    elif mode == "gate":
        r_ref, p_ref, wpe_ref = refs[pos:pos + 3]
        pos += 3
    o_ref = refs[pos]
    wbs = refs[pos + 1:pos + 1 + n_x]

    @pl.when(pl.program_id(1) == 0)
    def _():
        if w_shift:
            keep = wbs[0].shape[0] - w_shift
            wbs[0][:keep, :] = ws[0][w_shift:, :].astype(BF16)
            wbs[0][keep:, :] = w_next_ref[:w_shift, :].astype(BF16)
        else:
            for w_ref, wb_ref in zip(ws, wbs):
                wb_ref[...] = w_ref[...].astype(BF16)

    contract = (((1,), (1,)), ((), ())) if w_t else (((1,), (0,)), ((), ()))
    tm, tn = o_ref.shape
    halves = {"rope": 4, "gate": 2}.get(mode, 1)
    halves = halves if tm % (halves * 2 * SUBLANES) == 0 else 1
    rows_per = tm // halves
    for hf in range(halves):
        rs = slice(hf * rows_per, (hf + 1) * rows_per)
        acc = None
        for x_ref, wb_ref in zip(xs, wbs):
            part = lax.dot_general(x_ref[rs, :], wb_ref[...], contract, preferred_element_type=F32)
            acc = part if acc is None else acc + part

        if mode == "rope":
            rep = tn // HEAD_DIM
            acc = (acc * jnp.tile(c_ref[rs, :], (1, rep))
                   + pltpu.roll(acc, tn - ROPE_HALF, 1) * jnp.tile(s1_ref[rs, :], (1, rep))
                   + pltpu.roll(acc, ROPE_HALF, 1) * jnp.tile(s2_ref[rs, :], (1, rep)))
        elif mode == "relu2":
            acc = jnp.square(jnp.maximum(acc, 0.0))
        elif mode == "resid":
            acc = alpha * r_ref[rs, :] + acc
        elif mode == "gate":
            emb = jnp.dot(p_ref[rs, :].astype(BF16), wpe_ref[...].astype(BF16), preferred_element_type=F32)
            acc = alpha * r_ref[rs, :] + jax.nn.sigmoid(acc) * emb
        o_ref[rs, :] = acc.astype(o_ref.dtype)


def _mmw(xs, w_specs, layer, n, out_dtype, mode="plain", extra_inputs=(), extra_specs=(), alpha=1.0, tn=MM_TN,
         w_shift=0, w_t=False):
    t = xs[0].shape[0]
    tm = _tile(t, MM_TM)
    assert n % tn == 0
    in_specs = [pl.BlockSpec((tm, x.shape[1]), lambda j, i: (i, 0)) for x in xs]
    w_arrays = [w for w, _, _, _ in w_specs]

    def w_spec(kdim, kblk, cfn, off=0, **kw):
        if w_t:
            return pl.BlockSpec((None, tn, kdim), lambda j, i: (layer, cfn(j) + off, kblk), **kw)
        return pl.BlockSpec((None, kdim, tn), lambda j, i: (layer, kblk, cfn(j) + off), **kw)

    for _, kdim, kblk, cfn in w_specs:
        in_specs.append(w_spec(kdim, kblk, cfn))
    if w_shift:
        assert w_t and len(w_specs) == 1 and w_shift % SUBLANES == 0
        w, kdim, kblk, cfn = w_specs[0]
        in_specs.append(w_spec(kdim, kblk, cfn, off=1, pipeline_mode=pl.Buffered(1)))
        w_arrays.append(w)
    return pl.pallas_call(
        functools.partial(_mmw_kernel, n_x=len(xs), mode=mode, alpha=alpha, w_shift=w_shift, w_t=w_t),
        grid=(n // tn, t // tm),
        in_specs=in_specs + list(extra_specs),
        out_specs=pl.BlockSpec((tm, tn), lambda j, i: (i, j)),
        out_shape=jax.ShapeDtypeStruct((t, n), out_dtype),
        scratch_shapes=[pltpu.VMEM((tn, kdim) if w_t else (kdim, tn), BF16) for _, kdim, _, _ in w_specs],
        compiler_params=_params(("arbitrary", "arbitrary")),
    )(*xs, *w_arrays, *extra_inputs)


def _mmk_kernel(x_ref, w_ref, r_ref, o_ref, wb_ref, *, alpha, row_chunk):
    k = pl.program_id(2)

    @pl.when(k == 0)
    def _():
        o_ref[...] = jnp.zeros_like(o_ref)

    wb_ref[...] = w_ref[...].astype(BF16)
    for c in range(o_ref.shape[0] // row_chunk):
        rs = slice(c * row_chunk, (c + 1) * row_chunk)
        o_ref[rs, :] += jnp.dot(x_ref[rs, :], wb_ref[...], preferred_element_type=F32)

    @pl.when(k == pl.num_programs(2) - 1)
    def _():
        o_ref[...] += alpha * r_ref[...]


def _mmk_resid(x, w, layer, resid, alpha):
    t, kdim = x.shape
    n = w.shape[2]
    tm, tn, tk = _tile(t, 2048), _tile(n, 1024), _tile(kdim, 1024)
    return pl.pallas_call(
        functools.partial(_mmk_kernel, alpha=alpha, row_chunk=_tile(tm, 512)),
        grid=(t // tm, n // tn, kdim // tk),
        in_specs=[pl.BlockSpec((tm, tk), lambda i, j, k: (i, k)),
                  pl.BlockSpec((None, tk, tn), lambda i, j, k: (layer, k, j)),
                  pl.BlockSpec((tm, tn), lambda i, j, k: (i, j))],
        out_specs=pl.BlockSpec((tm, tn), lambda i, j, k: (i, j)),
        out_shape=jax.ShapeDtypeStruct((t, n), F32),
        scratch_shapes=[pltpu.VMEM((tk, tn), BF16)],
        compiler_params=_params(("parallel", "parallel", "arbitrary")),
    )(x, w, resid)


def _stream_perm(tm, dil, inverse):
    n = tm // dil
    a = lax.broadcasted_iota(jnp.int32, (tm, tm), 0)
    b = lax.broadcasted_iota(jnp.int32, (tm, tm), 1)
    if inverse:
        src = jnp.bitwise_and(a, dil - 1) * n + lax.shift_right_logical(a, _log2(dil))
    else:
        src = jnp.bitwise_and(a, n - 1) * dil + lax.shift_right_logical(a, _log2(n))
    return b == src


def _ln_kernel(*refs, normalize, emit_f32, emit_bf16, emit_streams):
    y_ref, g_ref, b_ref = refs[:3]
    outs = list(refs[3:])
    f32_ref = outs.pop(0) if emit_f32 else None
    bf16_ref = outs.pop(0) if emit_bf16 else None
    assert emit_bf16 or not emit_streams
    tm = y_ref.shape[0]
    rows = _tile(tm, LN_ROW_CHUNK)

    def norm_rows(ri, carry):
        rs = pl.ds(pl.multiple_of(ri * rows, rows), rows)
        y = y_ref[rs, :]
        if normalize:
            mu = jnp.mean(y, axis=-1, keepdims=True)
            yc = y - mu
            var = jnp.mean(jnp.square(yc), axis=-1, keepdims=True)
            y = yc * lax.rsqrt(var + LN_EPS) * g_ref[...] + b_ref[...]
        if emit_f32:
            f32_ref[rs, :] = y
        if emit_bf16:
            bf16_ref[rs, :] = y.astype(BF16)
        return carry

    lax.fori_loop(0, tm // rows, norm_rows, 0)

    if emit_streams:
        y16 = bf16_ref[...]
        for dil in ATTN_DILATIONS[1:]:
            o_ref = outs.pop(0)
            perm = jnp.where(_stream_perm(tm, dil, inverse=False), 1.0, 0.0).astype(BF16)
            moved = jnp.dot(perm, y16, preferred_element_type=F32).astype(BF16)
            n = tm // dil
            for r in range(dil):
                o_ref[r] = moved[r * n:(r + 1) * n]


def _layer_norm(y3, gain, bias, normalize=True, emit_f32=True, emit_bf16=True, emit_streams=False):
    b, s, d = y3.shape
    tm = _tile(s, LN_TM if emit_streams else 2 * LN_TM)
    tok = pl.BlockSpec((None, tm, d), lambda bi, i: (bi, i, 0))
    vec = pl.BlockSpec((1, d), lambda bi, i: (0, 0))
    out_specs, out_shape = [], []
    if emit_f32:
        out_specs.append(tok)
        out_shape.append(jax.ShapeDtypeStruct((b, s, d), F32))
    if emit_bf16:
        out_specs.append(tok)
        out_shape.append(jax.ShapeDtypeStruct((b, s, d), BF16))
    if emit_streams:
        for dil in ATTN_DILATIONS[1:]:
            out_specs.append(pl.BlockSpec((None, dil, tm // dil, d), lambda bi, i: (bi, 0, i, 0)))
            out_shape.append(jax.ShapeDtypeStruct((b, dil, s // dil, d), BF16))
    return pl.pallas_call(
        functools.partial(_ln_kernel, normalize=normalize, emit_f32=emit_f32, emit_bf16=emit_bf16,
                          emit_streams=emit_streams),
        grid=(b, s // tm),
        in_specs=[tok, vec, vec],
        out_specs=out_specs,
        out_shape=out_shape,
        compiler_params=_params(("parallel", "parallel")),
    )(y3, gain.reshape(1, d), bias.reshape(1, d))


def _attn_kernel(q_ref, kc_ref, kp_ref, vc_ref, vp_ref, o_ref, l_ref, s_scr, p_scr, m_scr, bias_scr, *, tq, scale):
    n = pl.program_id(2)
    blk = ATTN_BLOCK
    units = [(h, i) for h in range(ATTN_HPG) for i in range(tq // blk)]
    nt = (((1,), (1,)), ((), ()))

    def keys_vals(ref_c, ref_p, h, i):
        hs = slice(h * HEAD_DIM, (h + 1) * HEAD_DIM)
        if i == 0:
            return jnp.concatenate([ref_p[:, hs], ref_c[:blk, hs]], axis=0)
        return ref_c[(i - 1) * blk:(i + 1) * blk, hs]

    row = lax.broadcasted_iota(jnp.int32, (blk, 2 * blk), 0)
    col = lax.broadcasted_iota(jnp.int32, (blk, 2 * blk), 1)
    bias_scr[...] = jnp.where(col < blk,
                              jnp.where(col >= row, 0.0, -jnp.inf),
                              jnp.where(col - blk <= row, 0.0, -jnp.inf))
    col_r = lax.broadcasted_iota(jnp.int32, (ATTN_ROWS, 2 * blk), 1)
    first_pen = jnp.where(col_r < blk, jnp.where(n == 0, -jnp.inf, 0.0), 0.0)

    for u, (h, i) in enumerate(units):
        q = q_ref[i * blk:(i + 1) * blk, h * HEAD_DIM:(h + 1) * HEAD_DIM]
        s_scr[u] = lax.dot_general(q, keys_vals(kc_ref, kp_ref, h, i), nt, preferred_element_type=F32)

    for u, (h, i) in enumerate(units):
        for r in range(blk // ATTN_ROWS):
            rs = slice(r * ATTN_ROWS, (r + 1) * ATTN_ROWS)
            s = s_scr[u, rs, :] * scale + bias_scr[rs, :]
            if i == 0:
                s = s + first_pen
            m = jnp.max(s, axis=-1, keepdims=True)
            p_scr[u, rs, :] = jnp.exp(s - m).astype(BF16)
            m_scr[u, rs, :] = jnp.broadcast_to(m, (ATTN_ROWS, HEAD_DIM))

    ones = jnp.ones((2 * blk, HEAD_DIM), BF16)
    for u, (h, i) in enumerate(units):
        v_aug = jnp.concatenate([keys_vals(vc_ref, vp_ref, h, i), ones], axis=1)
        pv = jnp.dot(p_scr[u], v_aug, preferred_element_type=F32)
        den = pv[:, HEAD_DIM:]
        rs, hs = slice(i * blk, (i + 1) * blk), slice(h * HEAD_DIM, (h + 1) * HEAD_DIM)
        o_ref[rs, hs] = pv[:, :HEAD_DIM] / den
        l_ref[rs, hs] = m_scr[u] + jnp.log(den)


def _attn_group(qkv):
    b, dil, sub, _ = qkv.shape
    tq = _tile(sub, 512)
    per = tq // ATTN_BLOCK
    n_units = ATTN_HPG * per

    def cur(c):
        return pl.BlockSpec((None, None, tq, D_AGRP), lambda bi, r, n: (bi, r, n, c))

    def prev(c):
        return pl.BlockSpec((None, None, ATTN_BLOCK, D_AGRP),
                            lambda bi, r, n: (bi, r, jnp.maximum(n * per - 1, 0), c))

    out_sds = jax.ShapeDtypeStruct((b, dil, sub, D_AGRP), F32)
    return pl.pallas_call(
        functools.partial(_attn_kernel, tq=tq, scale=HEAD_DIM ** -0.5),
        grid=(b, dil, sub // tq),
        in_specs=[cur(0), cur(1), prev(1), cur(2), prev(2)],
        out_specs=[cur(0), cur(0)],
        out_shape=[out_sds, out_sds],
        scratch_shapes=[pltpu.VMEM((n_units, ATTN_BLOCK, 2 * ATTN_BLOCK), F32),
                        pltpu.VMEM((n_units, ATTN_BLOCK, 2 * ATTN_BLOCK), BF16),
                        pltpu.VMEM((n_units, ATTN_BLOCK, HEAD_DIM), F32),
                        pltpu.VMEM((ATTN_BLOCK, 2 * ATTN_BLOCK), F32)],
        compiler_params=_params(("parallel", "parallel", "arbitrary")),
    )(qkv, qkv, qkv, qkv, qkv)


def _to_token_order(x_ref, dil):
    if dil == 1:
        return x_ref[0]
    n, w = x_ref.shape[1], x_ref.shape[2]
    x = x_ref[...].reshape(dil * n, w)
    perm = jnp.where(_stream_perm(dil * n, dil, inverse=True), 1.0, 0.0).astype(BF16)
    hi = x.astype(BF16)
    r1 = x - hi.astype(F32)
    mid = r1.astype(BF16)
    lo = (r1 - mid.astype(F32)).astype(BF16)
    return (jnp.dot(perm, hi, preferred_element_type=F32)
            + jnp.dot(perm, mid, preferred_element_type=F32)
            + jnp.dot(perm, lo, preferred_element_type=F32))


def _attn_mix_kernel(o0, o1, o2, l0, l1, l2, y_ref):
    os_ = [_to_token_order(o, d) for o, d in zip((o0, o1, o2), ATTN_DILATIONS)]
    ls = [_to_token_order(l, d) for l, d in zip((l0, l1, l2), ATTN_DILATIONS)]
    m = jnp.maximum(jnp.maximum(ls[0], ls[1]), ls[2])
    es = [jnp.exp(l - m) for l in ls]
    inv = 1.0 / (es[0] + es[1] + es[2])
    for g in range(len(ATTN_DILATIONS)):
        y_ref[:, g * D_AGRP:(g + 1) * D_AGRP] = (os_[g] * (es[g] * inv)).astype(y_ref.dtype)


def _attn_mix(os_, ls_):
    b, _, s, _ = os_[0].shape
    tm = _tile(s, LN_TM)
    specs = [pl.BlockSpec((None, d, tm // d, D_AGRP), lambda bi, i: (bi, 0, i, 0)) for d in ATTN_DILATIONS]
    return pl.pallas_call(
        _attn_mix_kernel,
        grid=(b, s // tm),
        in_specs=specs + specs,
        out_specs=pl.BlockSpec((None, tm, D_ATTN), lambda bi, i: (bi, i, 0)),
        out_shape=jax.ShapeDtypeStruct((b, s, D_ATTN), BF16),
        compiler_params=_params(("parallel", "parallel")),
    )(*os_, *ls_)


def _shift_rows(cur, halo, sh):
    rolled = pltpu.roll(cur, sh, 0)
    hr = pltpu.roll(halo, sh, 0)
    row = lax.broadcasted_iota(jnp.int32, halo.shape, 0)
    first = jnp.where(row < sh, hr, rolled[:SUBLANES])
    return jnp.concatenate([first, rolled[SUBLANES:]], axis=0)


def _causal_conv(cur, halo, w):
    kk = w.shape[0]
    acc = w[kk - 1:kk] * cur
    for sh in range(1, kk):
        acc = acc + w[kk - 1 - sh:kk - sh] * _shift_rows(cur, halo, sh)
    return acc


def _sc_kernel(b_ref, c_ref, h_ref, ch_ref, hh_ref, w_ref, o_ref):
    s = pl.program_id(1)
    g = c_ref[...] * h_ref[...]
    gh = ch_ref[...] * hh_ref[...] * jnp.where(s > 0, 1.0, 0.0)
    o_ref[...] = (b_ref[...] * _causal_conv(g, gh, w_ref[...])).astype(o_ref.dtype)


def _short_conv(sc, conv_w):
    b, s, _ = sc.shape
    ts = _tile(s, 512)
    per = ts // SUBLANES

    def cur(j):
        return pl.BlockSpec((None, ts, D_SC), lambda bi, si: (bi, si, j))

    def halo(j):
        return pl.BlockSpec((None, SUBLANES, D_SC), lambda bi, si: (bi, jnp.maximum(si * per - 1, 0), j))

    return pl.pallas_call(
        _sc_kernel,
        grid=(b, s // ts),
        in_specs=[cur(0), cur(1), cur(2), halo(1), halo(2),
                  pl.BlockSpec((SC_CONV, D_SC), lambda bi, si: (0, 0))],
        out_specs=pl.BlockSpec((None, ts, D_SC), lambda bi, si: (bi, si, 0)),
        out_shape=jax.ShapeDtypeStruct((b, s, D_SC), BF16),
        compiler_params=_params(("parallel", "arbitrary")),
    )(sc, sc, sc, sc, sc, conv_w)


def _ssd_kernel(xs_ref, z_ref, b_ref, c_ref, dt_ref,
                wx_ref, bx_ref, wb_ref, bb_ref, wc_ref, bc_ref,
                dtb_ref, alog_ref, dsk_ref, nw_ref,
                o_ref,
                h_ref, xp_ref, bp_ref, cp_ref, y_ref, *, tc):
    g = pl.program_id(1)
    s = pl.program_id(2)
    q = SSM_CHUNK
    p = SSM_HEAD_DIM
    pads = ((xs_ref, xp_ref, wx_ref, bx_ref), (b_ref, bp_ref, wb_ref, bb_ref), (c_ref, cp_ref, wc_ref, bc_ref))

    @pl.when(s == 0)
    def _():
        h_ref[...] = jnp.zeros_like(h_ref)
        for _, pad_ref, _, _ in pads:
            pad_ref[:SUBLANES, :] = jnp.zeros((SUBLANES, pad_ref.shape[1]), F32)

    for raw_ref, pad_ref, _, _ in pads:
        pad_ref[SUBLANES:, :] = raw_ref[...]

    dt_shift = (LANES - g * SSM_HPG) % LANES
    a_row = -jnp.exp(alog_ref[...])
    dskip = dsk_ref[...]
    li = lax.broadcasted_iota(jnp.int32, (q, q), 0)
    si = lax.broadcasted_iota(jnp.int32, (q, q), 1)
    causal = li >= si
    tri = jnp.where(causal, 1.0, 0.0).astype(F32)
    first_head = lax.broadcasted_iota(jnp.int32, (q, 2 * p), 1) < p
    nt = (((1,), (1,)), ((), ()))

    def conv_silu(pad_ref, w_ref, bias_ref, r0):
        w = w_ref[...]
        win = pad_ref[pl.ds(r0, q + SUBLANES), :]
        acc = bias_ref[...]
        for k in range(SSM_CONV):
            lo = SUBLANES - (SSM_CONV - 1) + k
            acc = acc + w[k:k + 1] * win[lo:lo + q]
        return acc * jax.nn.sigmoid(acc)

    def chunk(ci, carry):
        r0 = pl.multiple_of(ci * q, q)
        xq = conv_silu(xp_ref, wx_ref, bx_ref, r0)
        bq = conv_silu(bp_ref, wb_ref, bb_ref, r0)
        cq = conv_silu(cp_ref, wc_ref, bc_ref, r0)
        dt_raw = pltpu.roll(dt_ref[pl.ds(r0, q), :], dt_shift, 1)
        dtq = jax.nn.softplus(dt_raw + dtb_ref[...])
        cs = jnp.dot(tri, dtq * a_row, precision=lax.Precision.HIGHEST,
                     preferred_element_type=F32)
        cs_t = cs.T
        dt_t = dtq.T
        bb = bq.astype(BF16)
        cb16 = cq.astype(BF16)
        cb = lax.dot_general(cb16, bb, nt, preferred_element_type=F32)
        b_t = bq.T
        hprev = h_ref[...]
        y_off = jnp.dot(cb16, hprev.astype(BF16), preferred_element_type=F32)
        for jp in range(SSM_HPG // 2):
            cols = slice(jp * 2 * p, (jp + 1) * 2 * p)
            x2 = xq[:, cols]
            y_d, st, e_col, e_last = None, None, [], []
            for half in range(2):
                j = 2 * jp + half
                xh = (jnp.where(first_head, x2, 0.0) if half == 0 else jnp.where(first_head, 0.0, x2)).astype(BF16)
                col = cs[:, j:j + 1]
                row = cs_t[j:j + 1, :]
                dtrow = dt_t[j:j + 1, :]
                decay = jnp.exp(jnp.where(causal, col - row, -jnp.inf))
                m = (cb * decay * dtrow).astype(BF16)
                part = jnp.dot(m, xh, preferred_element_type=F32)
                y_d = part if y_d is None else y_d + part
                last = cs[q - 1:q, j:j + 1]
                w_row = jnp.exp(last - row) * dtrow
                part = jnp.dot((b_t * w_row).astype(BF16), xh, preferred_element_type=F32)
                st = part if st is None else st + part
                e_col.append(jnp.exp(col))
                e_last.append(jnp.exp(last))
            y_ref[:, cols] = (y_d + y_off[:, cols] * jnp.where(first_head, e_col[0], e_col[1])
                              + dskip[:, cols] * x2)
            h_ref[:, cols] = hprev[:, cols] * jnp.where(first_head[:1], e_last[0], e_last[1]) + st

        z = z_ref[pl.ds(r0, q), :]
        yv = y_ref[...] * (z * jax.nn.sigmoid(z))
        ms = jnp.mean(jnp.square(yv), axis=-1, keepdims=True)
        o_ref[pl.ds(r0, q), :] = (yv * lax.rsqrt(ms + RMS_EPS) * nw_ref[...]).astype(o_ref.dtype)
        return carry

    lax.fori_loop(0, tc // q, chunk, 0)

    for _, pad_ref, _, _ in pads:
        pad_ref[:SUBLANES, :] = pad_ref[tc:, :]


def _ssd(zx, dt, conv_w, conv_b, dt_bias, a_log, d_skip, norm_w):
    b, s, _ = zx.shape
    tc = _tile(s, 1024)
    gw, ns, hpg = SSM_GW, SSM_STATE, SSM_HPG

    def tok(width, base):
        return pl.BlockSpec((None, tc, width), lambda bi, g, si: (bi, si, base // width + g))

    def par(rows, width):
        return pl.BlockSpec((rows, width), lambda bi, g, si: (0, g))

    def grp(arr):
        a = arr.reshape(SSM_GROUPS, 1, hpg).astype(F32)
        return jnp.pad(a, ((0, 0), (0, 0), (0, LANES - hpg)))

    grp_spec = pl.BlockSpec((None, 1, LANES), lambda bi, g, si: (g, 0, 0))
    wx, wb, wc = conv_w[:, :D_SSM], conv_w[:, D_SSM:D_SSM + 4 * ns], conv_w[:, D_SSM + 4 * ns:]
    cb2 = conv_b.reshape(1, -1)
    bx, bb, bc = cb2[:, :D_SSM], cb2[:, D_SSM:D_SSM + 4 * ns], cb2[:, D_SSM + 4 * ns:]
    d_exp = jnp.repeat(d_skip.astype(F32), SSM_HEAD_DIM).reshape(1, D_SSM)
    return pl.pallas_call(
        functools.partial(_ssd_kernel, tc=tc),
        grid=(b, SSM_GROUPS, s // tc),
        in_specs=[
            tok(gw, ZX_XS), tok(gw, 0), tok(ns, ZX_B), tok(ns, ZX_C),
            pl.BlockSpec((None, tc, LANES), lambda bi, g, si: (bi, si, 0)),
            par(SSM_CONV, gw), par(1, gw),
            par(SSM_CONV, ns), par(1, ns),
            par(SSM_CONV, ns), par(1, ns),
            grp_spec, grp_spec,
            par(1, gw), par(1, gw),
        ],
        out_specs=pl.BlockSpec((None, tc, gw), lambda bi, g, si: (bi, si, g)),
        out_shape=jax.ShapeDtypeStruct((b, s, D_SSM), BF16),
        scratch_shapes=[
            pltpu.VMEM((ns, gw), F32),
            pltpu.VMEM((SUBLANES + tc, gw), F32),
            pltpu.VMEM((SUBLANES + tc, ns), F32),
            pltpu.VMEM((SUBLANES + tc, ns), F32),
            pltpu.VMEM((SSM_CHUNK, gw), F32),
        ],
        compiler_params=_params(("parallel", "parallel", "arbitrary")),
    )(zx, zx, zx, zx, dt,
      wx, bx, wb, bb, wc, bc,
      grp(dt_bias), grp(a_log), d_exp, norm_w.reshape(1, D_SSM).astype(F32))


def _rope_tables(s_len):
    pos = jnp.arange(s_len, dtype=jnp.int32)
    inv_freq = ROPE_THETA ** (-jnp.arange(ROPE_HALF, dtype=F32) / ROPE_HALF)
    ang = pos.astype(F32)[:, None] * inv_freq[None, :]
    cos, sin = jnp.cos(ang), jnp.sin(ang)
    rest = jnp.zeros((s_len, HEAD_DIM - 2 * ROPE_HALF), F32)
    z16 = jnp.zeros((s_len, ROPE_HALF), F32)
    c = jnp.concatenate([cos, cos, jnp.ones_like(rest)], axis=1)
    s1 = jnp.concatenate([-sin, z16, rest], axis=1)
    s2 = jnp.concatenate([z16, sin, rest], axis=1)
    ident = (jnp.ones_like(c), jnp.zeros_like(c), jnp.zeros_like(c))
    return tuple(jnp.stack([t, i]) for t, i in zip((c, s1, s2), ident))


def _stream_major_rows(tab, dil):
    k, s, w = tab.shape
    return tab.reshape(k, s // dil, dil, w).transpose(0, 2, 1, 3).reshape(k, s, w)


def kernel(x, p, w_in, ssm_conv_w, ssm_conv_b, ssm_dt_bias, ssm_a_log, ssm_d, ssm_norm_w, sc_conv_w,
           w_out, ln1_g, ln1_b, w_up, w_down, ln2_g, ln2_b, w_pe, w_gate, ln3_g, ln3_b):
    bsz, s_len, d_model = x.shape
    depth = w_in.shape[0]
    t = bsz * s_len
    d_ple = p.shape[-1]
    alpha = (2.0 * depth) ** 0.25
    tm = _tile(t, MM_TM)
    assert s_len % tm == 0
    rope_blocks = s_len // tm
    rope = _rope_tables(s_len)
    rope_specs = [pl.BlockSpec((None, tm, HEAD_DIM), lambda j, i: (j // 2, i % rope_blocks, 0))] * 3
    rope_by_dil = {d: tuple(_stream_major_rows(tab, d) for tab in rope) for d in ATTN_DILATIONS}
    n_grp = len(ATTN_DILATIONS)
    w_in_t = jnp.swapaxes(w_in, 1, 2)

    def flat(a):
        return a.reshape(t, a.shape[-1])

    xf = x
    xb, xb4, xb16 = _layer_norm(x, ln1_g[0], ln1_b[0], normalize=False, emit_f32=False, emit_streams=True)
    for i in range(depth):
        streams = {1: xb, 4: xb4, 16: xb16}
        outs = []
        for g, dil in enumerate(ATTN_DILATIONS):
            qkv = _mmw([flat(streams[dil])], [(w_in_t, d_model, 0, lambda j, g=g: j * n_grp + g)], i,
                       3 * D_AGRP, BF16, mode="rope", extra_inputs=rope_by_dil[dil], extra_specs=rope_specs,
                       w_t=True)
            outs.append(_attn_group(qkv.reshape(bsz, dil, s_len // dil, 3 * D_AGRP)))
        y_attn = _attn_mix([o for o, _ in outs], [l for _, l in outs])

        xbf = flat(xb)
        zx = _mmw([xbf], [(w_in_t, d_model, 0, lambda j: U_Z // MM_TN + j)], i, D_ZX, F32, w_t=True)
        dt = _mmw([xbf], [(w_in_t, d_model, 0, lambda j: U_DT // LANES)], i, LANES, F32, tn=LANES, w_t=True)
        sc = _mmw([xbf], [(w_in_t, d_model, 0, lambda j: U_DT // MM_TN + j)], i, 3 * D_SC, F32,
                  w_shift=U_SC - U_DT, w_t=True)
        y_ssm = _ssd(zx.reshape(bsz, s_len, D_ZX), dt.reshape(bsz, s_len, LANES), ssm_conv_w[i], ssm_conv_b[i],
                     ssm_dt_bias[i], ssm_a_log[i], ssm_d[i], ssm_norm_w[i])
        y_sc = _short_conv(sc.reshape(bsz, s_len, 3 * D_SC), sc_conv_w[i])

        resid_spec = pl.BlockSpec((tm, MM_TN), lambda j, i_: (i_, j))
        y1 = _mmw([flat(y_attn), flat(y_ssm), flat(y_sc)],
                  [(w_out, D_ATTN, 0, lambda j: j), (w_out, D_SSM, 1, lambda j: j),
                   (w_out, D_SC, (D_ATTN + D_SSM) // D_SC, lambda j: j)], i,
                  d_model, F32, mode="resid", extra_inputs=(flat(xf),), extra_specs=(resid_spec,), alpha=alpha)
        xf, xb = _layer_norm(y1.reshape(bsz, s_len, d_model), ln1_g[i], ln1_b[i])

        hid = _mmw([flat(xb)], [(w_up, d_model, 0, lambda j: j)], i, w_up.shape[2], BF16, mode="relu2")
        y2 = _mmk_resid(hid, w_down, i, flat(xf), alpha)
        xf, xb = _layer_norm(y2.reshape(bsz, s_len, d_model), ln2_g[i], ln2_b[i])

        y3 = _mmw([flat(xb)], [(w_gate, d_model, 0, lambda j: j)], i, d_model, F32, mode="gate",
                  extra_inputs=(flat(xf), flat(p[i]), w_pe[i]),
                  extra_specs=(resid_spec, pl.BlockSpec((tm, d_ple), lambda j, i_: (i_, 0)),
                               pl.BlockSpec((d_ple, MM_TN), lambda j, i_: (0, j))),
                  alpha=alpha)
        last = i == depth - 1
        res = _layer_norm(y3.reshape(bsz, s_len, d_model), ln3_g[i], ln3_b[i],
                          emit_bf16=not last, emit_streams=not last)
        if last:
            xf, = res
        else:
            xf, xb, xb4, xb16 = res
    return xf
```

```python
import functools

import jax
import jax.numpy as jnp
from jax import lax
from jax.experimental import pallas as pl
from jax.experimental.pallas import tpu as pltpu

HEAD_DIM = 128
ATTN_DILATIONS = (1, 4, 16)
ATTN_HPG = 4
ATTN_BLOCK = 128
ATTN_ROWS = 32
D_ATTN = ATTN_HPG * len(ATTN_DILATIONS) * HEAD_DIM
D_AGRP = ATTN_HPG * HEAD_DIM
ROPE_THETA = 500000.0
ROPE_HALF = HEAD_DIM // 8
D_SSM = 1536
SSM_HEAD_DIM = 64
SSM_HEADS = D_SSM // SSM_HEAD_DIM
SSM_GROUPS = 4
SSM_HPG = SSM_HEADS // SSM_GROUPS
SSM_GW = D_SSM // SSM_GROUPS
SSM_STATE = 128
SSM_CONV = 4
SSM_CHUNK = 128
D_SC = 1024
SC_CONV = 3
LN_EPS = 1e-5
RMS_EPS = 1e-5

U_Z = 3 * D_ATTN
U_DT = U_Z + 2 * D_SSM + 2 * SSM_GROUPS * SSM_STATE
U_SC = U_DT + SSM_HEADS
D_ZX = U_DT - U_Z
ZX_XS = D_SSM
ZX_B = 2 * D_SSM
ZX_C = ZX_B + SSM_GROUPS * SSM_STATE

LANES = 128
SUBLANES = 8
VMEM_LIMIT_BYTES = 56 * 1024 * 1024

BF16 = jnp.bfloat16
F32 = jnp.float32

MM_TM = 1024
MM_TN = 512
LN_TM = 256
LN_ROW_CHUNK = 32


def _params(sem):
    return pltpu.CompilerParams(dimension_semantics=sem, vmem_limit_bytes=VMEM_LIMIT_BYTES)


def _tile(n, pref):
    t = min(n, pref)
    while n % t:
        t -= 1
    return t


def _log2(n):
    assert n > 0 and n & (n - 1) == 0, n
    return n.bit_length() - 1


def _resid_term(r_refs, rs, alpha):
    if len(r_refs) == 1:
        return alpha * r_refs[0][rs, :]
    y_ref, mu_ref, rstd_ref, ag_ref, ab_ref = r_refs
    rep = y_ref.shape[1] // LANES
    mu = jnp.tile(mu_ref[rs, :], (1, rep))
    rstd = jnp.tile(rstd_ref[rs, :], (1, rep))
    return (y_ref[rs, :] - mu) * rstd * ag_ref[...] + ab_ref[...]


def _resid_operands(resid, alpha, tm, tn, index):
    if len(resid) == 1:
        return [resid[0]], [pl.BlockSpec((tm, tn), index)]
    y, mu, rstd, gain, bias = resid
    d = y.shape[1]

    def rows(*g):
        return (index(*g)[0], 0)

    def cols(*g):
        return (0, index(*g)[1])

    return ([y, mu, rstd, (alpha * gain).reshape(1, d), (alpha * bias).reshape(1, d)],
            [pl.BlockSpec((tm, tn), index), pl.BlockSpec((tm, LANES), rows), pl.BlockSpec((tm, LANES), rows),
             pl.BlockSpec((1, tn), cols), pl.BlockSpec((1, tn), cols)])


def _mmw_kernel(*refs, n_x, mode, alpha, w_shift, w_t, n_r):
    xs, ws = refs[:n_x], refs[n_x:2 * n_x]
    pos = 2 * n_x
    if w_shift:
        w_next_ref = refs[pos]
        pos += 1
    if mode == "rope":
        c_ref, s1_ref, s2_ref = refs[pos:pos + 3]
        pos += 3
    elif mode == "resid":
        r_refs = refs[pos:pos + n_r]
        pos += n_r
    elif mode == "gate":
        r_refs = refs[pos:pos + n_r]
        p_ref, wpe_ref = refs[pos + n_r:pos + n_r + 2]
        pos += n_r + 2
    o_ref = refs[pos]
    wbs = refs[pos + 1:pos + 1 + n_x]

    @pl.when(pl.program_id(1) == 0)
    def _():
        if w_shift:
            keep = wbs[0].shape[0] - w_shift
            wbs[0][:keep, :] = ws[0][w_shift:, :].astype(BF16)
            wbs[0][keep:, :] = w_next_ref[:w_shift, :].astype(BF16)
        else:
            for w_ref, wb_ref in zip(ws, wbs):
                wb_ref[...] = w_ref[...].astype(BF16)

    contract = (((1,), (1,)), ((), ())) if w_t else (((1,), (0,)), ((), ()))
    tm, tn = o_ref.shape
    halves = {"rope": 4, "gate": 2, "resid": 2}.get(mode, 1)
    halves = halves if tm % (halves * 2 * SUBLANES) == 0 else 1
    rows_per = tm // halves
    for hf in range(halves):
        rs = slice(hf * rows_per, (hf + 1) * rows_per)
        acc = None
        for x_ref, wb_ref in zip(xs, wbs):
            part = lax.dot_general(x_ref[rs, :], wb_ref[...], contract, preferred_element_type=F32)
            acc = part if acc is None else acc + part

        if mode == "rope":
            rep = tn // HEAD_DIM
            acc = (acc * jnp.tile(c_ref[rs, :], (1, rep))
                   + pltpu.roll(acc, tn - ROPE_HALF, 1) * jnp.tile(s1_ref[rs, :], (1, rep))
                   + pltpu.roll(acc, ROPE_HALF, 1) * jnp.tile(s2_ref[rs, :], (1, rep)))
        elif mode == "relu2":
            acc = jnp.square(jnp.maximum(acc, 0.0))
        elif mode == "resid":
            acc = _resid_term(r_refs, rs, alpha) + acc
        elif mode == "gate":
            emb = jnp.dot(p_ref[rs, :].astype(BF16), wpe_ref[...].astype(BF16), preferred_element_type=F32)
            acc = _resid_term(r_refs, rs, alpha) + jax.nn.sigmoid(acc) * emb
        o_ref[rs, :] = acc.astype(o_ref.dtype)


def _mmw(xs, w_specs, layer, n, out_dtype, mode="plain", extra_inputs=(), extra_specs=(), alpha=1.0, tn=MM_TN,
         w_shift=0, w_t=False, resid=None):
    t = xs[0].shape[0]
    tm = _tile(t, MM_TM)
    assert n % tn == 0
    in_specs = [pl.BlockSpec((tm, x.shape[1]), lambda j, i: (i, 0)) for x in xs]
    w_arrays = [w for w, _, _, _ in w_specs]

    def w_spec(kdim, kblk, cfn, off=0, **kw):
        if w_t:
            return pl.BlockSpec((None, tn, kdim), lambda j, i: (layer, cfn(j) + off, kblk), **kw)
        return pl.BlockSpec((None, kdim, tn), lambda j, i: (layer, kblk, cfn(j) + off), **kw)

    for _, kdim, kblk, cfn in w_specs:
        in_specs.append(w_spec(kdim, kblk, cfn))
    if w_shift:
        assert w_t and len(w_specs) == 1 and w_shift % SUBLANES == 0
        w, kdim, kblk, cfn = w_specs[0]
        in_specs.append(w_spec(kdim, kblk, cfn, off=1, pipeline_mode=pl.Buffered(1)))
        w_arrays.append(w)
    n_r = 0
    if resid is not None:
        r_arrays, r_specs = _resid_operands(resid, alpha, tm, tn, lambda j, i: (i, j))
        n_r = len(r_arrays)
        extra_inputs = tuple(r_arrays) + tuple(extra_inputs)
        extra_specs = tuple(r_specs) + tuple(extra_specs)
    return pl.pallas_call(
        functools.partial(_mmw_kernel, n_x=len(xs), mode=mode, alpha=alpha, w_shift=w_shift, w_t=w_t, n_r=n_r),
        grid=(n // tn, t // tm),
        in_specs=in_specs + list(extra_specs),
        out_specs=pl.BlockSpec((tm, tn), lambda j, i: (i, j)),
        out_shape=jax.ShapeDtypeStruct((t, n), out_dtype),
        scratch_shapes=[pltpu.VMEM((tn, kdim) if w_t else (kdim, tn), BF16) for _, kdim, _, _ in w_specs],
        compiler_params=_params(("arbitrary", "arbitrary")),
    )(*xs, *w_arrays, *extra_inputs)


def _mmk_kernel(x_ref, w_ref, *refs, alpha, row_chunk):
    r_refs, o_ref, wb_ref = refs[:-2], refs[-2], refs[-1]
    k = pl.program_id(2)

    @pl.when(k == 0)
    def _():
        o_ref[...] = jnp.zeros_like(o_ref)

    wb_ref[...] = w_ref[...].astype(BF16)
    for c in range(o_ref.shape[0] // row_chunk):
        rs = slice(c * row_chunk, (c + 1) * row_chunk)
        o_ref[rs, :] += jnp.dot(x_ref[rs, :], wb_ref[...], preferred_element_type=F32)

    @pl.when(k == pl.num_programs(2) - 1)
    def _():
        for c in range(o_ref.shape[0] // row_chunk):
            rs = slice(c * row_chunk, (c + 1) * row_chunk)
            o_ref[rs, :] += _resid_term(r_refs, rs, alpha)


def _mmk_resid(x, w, layer, resid, alpha):
    t, kdim = x.shape
    n = w.shape[2]
    tm, tn, tk = _tile(t, 2048), _tile(n, 1024), _tile(kdim, 1024)
    r_arrays, r_specs = _resid_operands(resid, alpha, tm, tn, lambda i, j, k: (i, j))
    return pl.pallas_call(
        functools.partial(_mmk_kernel, alpha=alpha, row_chunk=_tile(tm, 512)),
        grid=(t // tm, n // tn, kdim // tk),
        in_specs=[pl.BlockSpec((tm, tk), lambda i, j, k: (i, k)),
                  pl.BlockSpec((None, tk, tn), lambda i, j, k: (layer, k, j)),
                  *r_specs],
        out_specs=pl.BlockSpec((tm, tn), lambda i, j, k: (i, j)),
        out_shape=jax.ShapeDtypeStruct((t, n), F32),
        scratch_shapes=[pltpu.VMEM((tk, tn), BF16)],
        compiler_params=_params(("parallel", "parallel", "arbitrary")),
    )(x, w, *r_arrays)


def _stream_perm(tm, dil, inverse):
    n = tm // dil
    a = lax.broadcasted_iota(jnp.int32, (tm, tm), 0)
    b = lax.broadcasted_iota(jnp.int32, (tm, tm), 1)
    if inverse:
        src = jnp.bitwise_and(a, dil - 1) * n + lax.shift_right_logical(a, _log2(dil))
    else:
        src = jnp.bitwise_and(a, n - 1) * dil + lax.shift_right_logical(a, _log2(n))
    return b == src


def _ln_kernel(*refs, normalize, emit_f32, emit_bf16, emit_streams):
    y_ref, g_ref, b_ref = refs[:3]
    outs = list(refs[3:])
    f32_ref = outs.pop(0) if emit_f32 else None
    bf16_ref = outs.pop(0) if emit_bf16 else None
    emit_stats = normalize and not emit_f32
    if emit_stats:
        mu_ref, rstd_ref = outs.pop(0), outs.pop(0)
    assert emit_bf16 or not emit_streams
    tm = y_ref.shape[0]
    rows = _tile(tm, LN_ROW_CHUNK)

    def norm_rows(ri, carry):
        rs = pl.ds(pl.multiple_of(ri * rows, rows), rows)
        y = y_ref[rs, :]
        if normalize:
            mu = jnp.mean(y, axis=-1, keepdims=True)
            yc = y - mu
            var = jnp.mean(jnp.square(yc), axis=-1, keepdims=True)
            rstd = lax.rsqrt(var + LN_EPS)
            y = yc * rstd * g_ref[...] + b_ref[...]
            if emit_stats:
                mu_ref[rs, :] = jnp.broadcast_to(mu, (rows, LANES))
                rstd_ref[rs, :] = jnp.broadcast_to(rstd, (rows, LANES))
        if emit_f32:
            f32_ref[rs, :] = y
        if emit_bf16:
            bf16_ref[rs, :] = y.astype(BF16)
        return carry

    lax.fori_loop(0, tm // rows, norm_rows, 0)

    if emit_streams:
        y16 = bf16_ref[...]
        for dil in ATTN_DILATIONS[1:]:
            o_ref = outs.pop(0)
            perm = jnp.where(_stream_perm(tm, dil, inverse=False), 1.0, 0.0).astype(BF16)
            moved = jnp.dot(perm, y16, preferred_element_type=F32).astype(BF16)
            n = tm // dil
            for r in range(dil):
                o_ref[r] = moved[r * n:(r + 1) * n]


def _layer_norm(y3, gain, bias, normalize=True, emit_f32=True, emit_bf16=True, emit_streams=False):
    b, s, d = y3.shape
    tm = _tile(s, LN_TM if emit_streams else 2 * LN_TM)
    tok = pl.BlockSpec((None, tm, d), lambda bi, i: (bi, i, 0))
    vec = pl.BlockSpec((1, d), lambda bi, i: (0, 0))
    out_specs, out_shape = [], []
    if emit_f32:
        out_specs.append(tok)
        out_shape.append(jax.ShapeDtypeStruct((b, s, d), F32))
    if emit_bf16:
        out_specs.append(tok)
        out_shape.append(jax.ShapeDtypeStruct((b, s, d), BF16))
    if normalize and not emit_f32:
        for _ in range(2):
            out_specs.append(pl.BlockSpec((None, tm, LANES), lambda bi, i: (bi, i, 0)))
            out_shape.append(jax.ShapeDtypeStruct((b, s, LANES), F32))
    if emit_streams:
        for dil in ATTN_DILATIONS[1:]:
            out_specs.append(pl.BlockSpec((None, dil, tm // dil, d), lambda bi, i: (bi, 0, i, 0)))
            out_shape.append(jax.ShapeDtypeStruct((b, dil, s // dil, d), BF16))
    return pl.pallas_call(
        functools.partial(_ln_kernel, normalize=normalize, emit_f32=emit_f32, emit_bf16=emit_bf16,
                          emit_streams=emit_streams),
        grid=(b, s // tm),
        in_specs=[tok, vec, vec],
        out_specs=out_specs,
        out_shape=out_shape,
        compiler_params=_params(("parallel", "parallel")),
    )(y3, gain.reshape(1, d), bias.reshape(1, d))


def _attn_kernel(q_ref, kc_ref, kp_ref, vc_ref, vp_ref, o_ref, l_ref, s_scr, p_scr, m_scr, bias_scr, *, tq, scale):
    n = pl.program_id(2)
    blk = ATTN_BLOCK
    units = [(h, i) for h in range(ATTN_HPG) for i in range(tq // blk)]
    nt = (((1,), (1,)), ((), ()))

    def keys_vals(ref_c, ref_p, h, i):
        hs = slice(h * HEAD_DIM, (h + 1) * HEAD_DIM)
        if i == 0:
            return jnp.concatenate([ref_p[:, hs], ref_c[:blk, hs]], axis=0)
        return ref_c[(i - 1) * blk:(i + 1) * blk, hs]

    row = lax.broadcasted_iota(jnp.int32, (blk, 2 * blk), 0)
    col = lax.broadcasted_iota(jnp.int32, (blk, 2 * blk), 1)
    bias_scr[...] = jnp.where(col < blk,
                              jnp.where(col >= row, 0.0, -jnp.inf),
                              jnp.where(col - blk <= row, 0.0, -jnp.inf))
    col_r = lax.broadcasted_iota(jnp.int32, (ATTN_ROWS, 2 * blk), 1)
    first_pen = jnp.where(col_r < blk, jnp.where(n == 0, -jnp.inf, 0.0), 0.0)

    for u, (h, i) in enumerate(units):
        q = q_ref[i * blk:(i + 1) * blk, h * HEAD_DIM:(h + 1) * HEAD_DIM]
        s_scr[u] = lax.dot_general(q, keys_vals(kc_ref, kp_ref, h, i), nt, preferred_element_type=F32)

    for u, (h, i) in enumerate(units):
        for r in range(blk // ATTN_ROWS):
            rs = slice(r * ATTN_ROWS, (r + 1) * ATTN_ROWS)
            s = s_scr[u, rs, :] * scale + bias_scr[rs, :]
            if i == 0:
                s = s + first_pen
            m = jnp.max(s, axis=-1, keepdims=True)
            p_scr[u, rs, :] = jnp.exp(s - m).astype(BF16)
            m_scr[u, rs, :] = jnp.broadcast_to(m, (ATTN_ROWS, HEAD_DIM))

    ones = jnp.ones((2 * blk, HEAD_DIM), BF16)
    for u, (h, i) in enumerate(units):
        v_aug = jnp.concatenate([keys_vals(vc_ref, vp_ref, h, i), ones], axis=1)
        pv = jnp.dot(p_scr[u], v_aug, preferred_element_type=F32)
        den = pv[:, HEAD_DIM:]
        rs, hs = slice(i * blk, (i + 1) * blk), slice(h * HEAD_DIM, (h + 1) * HEAD_DIM)
        o_ref[rs, hs] = pv[:, :HEAD_DIM] / den
        l_ref[rs, hs] = m_scr[u] + jnp.log(den)


def _attn_group(qkv):
    b, dil, sub, _ = qkv.shape
    tq = _tile(sub, 512)
    per = tq // ATTN_BLOCK
    n_units = ATTN_HPG * per

    def cur(c):
        return pl.BlockSpec((None, None, tq, D_AGRP), lambda bi, r, n: (bi, r, n, c))

    def prev(c):
        return pl.BlockSpec((None, None, ATTN_BLOCK, D_AGRP),
                            lambda bi, r, n: (bi, r, jnp.maximum(n * per - 1, 0), c))

    out_sds = jax.ShapeDtypeStruct((b, dil, sub, D_AGRP), F32)
    return pl.pallas_call(
        functools.partial(_attn_kernel, tq=tq, scale=HEAD_DIM ** -0.5),
        grid=(b, dil, sub // tq),
        in_specs=[cur(0), cur(1), prev(1), cur(2), prev(2)],
        out_specs=[cur(0), cur(0)],
        out_shape=[out_sds, out_sds],
        scratch_shapes=[pltpu.VMEM((n_units, ATTN_BLOCK, 2 * ATTN_BLOCK), F32),
                        pltpu.VMEM((n_units, ATTN_BLOCK, 2 * ATTN_BLOCK), BF16),
                        pltpu.VMEM((n_units, ATTN_BLOCK, HEAD_DIM), F32),
                        pltpu.VMEM((ATTN_BLOCK, 2 * ATTN_BLOCK), F32)],
        compiler_params=_params(("parallel", "parallel", "arbitrary")),
    )(qkv, qkv, qkv, qkv, qkv)


def _to_token_order(x_ref, dil):
    if dil == 1:
        return x_ref[0]
    n, w = x_ref.shape[1], x_ref.shape[2]
    x = x_ref[...].reshape(dil * n, w)
    perm = jnp.where(_stream_perm(dil * n, dil, inverse=True), 1.0, 0.0).astype(BF16)
    hi = x.astype(BF16)
    r1 = x - hi.astype(F32)
    mid = r1.astype(BF16)
    lo = (r1 - mid.astype(F32)).astype(BF16)
    return (jnp.dot(perm, hi, preferred_element_type=F32)
            + jnp.dot(perm, mid, preferred_element_type=F32)
            + jnp.dot(perm, lo, preferred_element_type=F32))


def _attn_mix_kernel(o0, o1, o2, l0, l1, l2, y_ref):
    os_ = [_to_token_order(o, d) for o, d in zip((o0, o1, o2), ATTN_DILATIONS)]
    ls = [_to_token_order(l, d) for l, d in zip((l0, l1, l2), ATTN_DILATIONS)]
    m = jnp.maximum(jnp.maximum(ls[0], ls[1]), ls[2])
    es = [jnp.exp(l - m) for l in ls]
    inv = 1.0 / (es[0] + es[1] + es[2])
    for g in range(len(ATTN_DILATIONS)):
        y_ref[:, g * D_AGRP:(g + 1) * D_AGRP] = (os_[g] * (es[g] * inv)).astype(y_ref.dtype)


def _attn_mix(os_, ls_):
    b, _, s, _ = os_[0].shape
    tm = _tile(s, LN_TM)
    specs = [pl.BlockSpec((None, d, tm // d, D_AGRP), lambda bi, i: (bi, 0, i, 0)) for d in ATTN_DILATIONS]
    return pl.pallas_call(
        _attn_mix_kernel,
        grid=(b, s // tm),
        in_specs=specs + specs,
        out_specs=pl.BlockSpec((None, tm, D_ATTN), lambda bi, i: (bi, i, 0)),
        out_shape=jax.ShapeDtypeStruct((b, s, D_ATTN), BF16),
        compiler_params=_params(("parallel", "parallel")),
    )(*os_, *ls_)


def _shift_rows(cur, halo, sh):
    rolled = pltpu.roll(cur, sh, 0)
    hr = pltpu.roll(halo, sh, 0)
    row = lax.broadcasted_iota(jnp.int32, halo.shape, 0)
    first = jnp.where(row < sh, hr, rolled[:SUBLANES])
    return jnp.concatenate([first, rolled[SUBLANES:]], axis=0)


def _causal_conv(cur, halo, w):
    kk = w.shape[0]
    acc = w[kk - 1:kk] * cur
    for sh in range(1, kk):
        acc = acc + w[kk - 1 - sh:kk - sh] * _shift_rows(cur, halo, sh)
    return acc


def _sc_kernel(b_ref, c_ref, h_ref, ch_ref, hh_ref, w_ref, o_ref):
    s = pl.program_id(1)
    g = c_ref[...] * h_ref[...]
    gh = ch_ref[...] * hh_ref[...] * jnp.where(s > 0, 1.0, 0.0)
    o_ref[...] = (b_ref[...] * _causal_conv(g, gh, w_ref[...])).astype(o_ref.dtype)


def _short_conv(sc, conv_w):
    b, s, _ = sc.shape
    ts = _tile(s, 512)
    per = ts // SUBLANES

    def cur(j):
        return pl.BlockSpec((None, ts, D_SC), lambda bi, si: (bi, si, j))

    def halo(j):
        return pl.BlockSpec((None, SUBLANES, D_SC), lambda bi, si: (bi, jnp.maximum(si * per - 1, 0), j))

    return pl.pallas_call(
        _sc_kernel,
        grid=(b, s // ts),
        in_specs=[cur(0), cur(1), cur(2), halo(1), halo(2),
                  pl.BlockSpec((SC_CONV, D_SC), lambda bi, si: (0, 0))],
        out_specs=pl.BlockSpec((None, ts, D_SC), lambda bi, si: (bi, si, 0)),
        out_shape=jax.ShapeDtypeStruct((b, s, D_SC), BF16),
        compiler_params=_params(("parallel", "arbitrary")),
    )(sc, sc, sc, sc, sc, conv_w)


def _ssd_kernel(xs_ref, z_ref, b_ref, c_ref, dt_ref,
                wx_ref, bx_ref, wb_ref, bb_ref, wc_ref, bc_ref,
                dtb_ref, alog_ref, dsk_ref, nw_ref,
                o_ref,
                h_ref, xp_ref, bp_ref, cp_ref, y_ref, *, tc):
    g = pl.program_id(1)
    s = pl.program_id(2)
    q = SSM_CHUNK
    p = SSM_HEAD_DIM
    pads = ((xs_ref, xp_ref, wx_ref, bx_ref), (b_ref, bp_ref, wb_ref, bb_ref), (c_ref, cp_ref, wc_ref, bc_ref))

    @pl.when(s == 0)
    def _():
        h_ref[...] = jnp.zeros_like(h_ref)
        for _, pad_ref, _, _ in pads:
            pad_ref[:SUBLANES, :] = jnp.zeros((SUBLANES, pad_ref.shape[1]), F32)

    for raw_ref, pad_ref, _, _ in pads:
        pad_ref[SUBLANES:, :] = raw_ref[...]

    dt_shift = (LANES - g * SSM_HPG) % LANES
    a_row = -jnp.exp(alog_ref[...])
    dskip = dsk_ref[...]
    li = lax.broadcasted_iota(jnp.int32, (q, q), 0)
    si = lax.broadcasted_iota(jnp.int32, (q, q), 1)
    causal = li >= si
    tri = jnp.where(causal, 1.0, 0.0).astype(F32)
    first_head = lax.broadcasted_iota(jnp.int32, (q, 2 * p), 1) < p
    nt = (((1,), (1,)), ((), ()))

    def conv_silu(pad_ref, w_ref, bias_ref, r0):
        w = w_ref[...]
        win = pad_ref[pl.ds(r0, q + SUBLANES), :]
        acc = bias_ref[...]
        for k in range(SSM_CONV):
            lo = SUBLANES - (SSM_CONV - 1) + k
            acc = acc + w[k:k + 1] * win[lo:lo + q]
        return acc * jax.nn.sigmoid(acc)

    def chunk(ci, carry):
        r0 = pl.multiple_of(ci * q, q)
        xq = conv_silu(xp_ref, wx_ref, bx_ref, r0)
        bq = conv_silu(bp_ref, wb_ref, bb_ref, r0)
        cq = conv_silu(cp_ref, wc_ref, bc_ref, r0)
        dt_raw = pltpu.roll(dt_ref[pl.ds(r0, q), :], dt_shift, 1)
        dtq = jax.nn.softplus(dt_raw + dtb_ref[...])
        cs = jnp.dot(tri, dtq * a_row, precision=lax.Precision.HIGHEST,
                     preferred_element_type=F32)
        cs_t = cs.T
        dt_t = dtq.T
        bb = bq.astype(BF16)
        cb16 = cq.astype(BF16)
        cb = lax.dot_general(cb16, bb, nt, preferred_element_type=F32)
        b_t = bq.T
        hprev = h_ref[...]
        y_off = jnp.dot(cb16, hprev.astype(BF16), preferred_element_type=F32)
        for jp in range(SSM_HPG // 2):
            cols = slice(jp * 2 * p, (jp + 1) * 2 * p)
            x2 = xq[:, cols]
            y_d, st, e_col, e_last = None, None, [], []
            for half in range(2):
                j = 2 * jp + half
                xh = (jnp.where(first_head, x2, 0.0) if half == 0 else jnp.where(first_head, 0.0, x2)).astype(BF16)
                col = cs[:, j:j + 1]
                row = cs_t[j:j + 1, :]
                dtrow = dt_t[j:j + 1, :]
                decay = jnp.exp(jnp.where(causal, col - row, -jnp.inf))
                m = (cb * decay * dtrow).astype(BF16)
                part = jnp.dot(m, xh, preferred_element_type=F32)
                y_d = part if y_d is None else y_d + part
                last = cs[q - 1:q, j:j + 1]
                w_row = jnp.exp(last - row) * dtrow
                part = jnp.dot((b_t * w_row).astype(BF16), xh, preferred_element_type=F32)
                st = part if st is None else st + part
                e_col.append(jnp.exp(col))
                e_last.append(jnp.exp(last))
            y_ref[:, cols] = (y_d + y_off[:, cols] * jnp.where(first_head, e_col[0], e_col[1])
                              + dskip[:, cols] * x2)
            h_ref[:, cols] = hprev[:, cols] * jnp.where(first_head[:1], e_last[0], e_last[1]) + st

        z = z_ref[pl.ds(r0, q), :]
        yv = y_ref[...] * (z * jax.nn.sigmoid(z))
        ms = jnp.mean(jnp.square(yv), axis=-1, keepdims=True)
        o_ref[pl.ds(r0, q), :] = (yv * lax.rsqrt(ms + RMS_EPS) * nw_ref[...]).astype(o_ref.dtype)
        return carry

    lax.fori_loop(0, tc // q, chunk, 0)

    for _, pad_ref, _, _ in pads:
        pad_ref[:SUBLANES, :] = pad_ref[tc:, :]


def _ssd(zx, dt, conv_w, conv_b, dt_bias, a_log, d_skip, norm_w):
    b, s, _ = zx.shape
    tc = _tile(s, 1024)
    gw, ns, hpg = SSM_GW, SSM_STATE, SSM_HPG

    def tok(width, base):
        return pl.BlockSpec((None, tc, width), lambda bi, g, si: (bi, si, base // width + g))

    def par(rows, width):
        return pl.BlockSpec((rows, width), lambda bi, g, si: (0, g))

    def grp(arr):
        a = arr.reshape(SSM_GROUPS, 1, hpg).astype(F32)
        return jnp.pad(a, ((0, 0), (0, 0), (0, LANES - hpg)))

    grp_spec = pl.BlockSpec((None, 1, LANES), lambda bi, g, si: (g, 0, 0))
    wx, wb, wc = conv_w[:, :D_SSM], conv_w[:, D_SSM:D_SSM + 4 * ns], conv_w[:, D_SSM + 4 * ns:]
    cb2 = conv_b.reshape(1, -1)
    bx, bb, bc = cb2[:, :D_SSM], cb2[:, D_SSM:D_SSM + 4 * ns], cb2[:, D_SSM + 4 * ns:]
    d_exp = jnp.repeat(d_skip.astype(F32), SSM_HEAD_DIM).reshape(1, D_SSM)
    return pl.pallas_call(
        functools.partial(_ssd_kernel, tc=tc),
        grid=(b, SSM_GROUPS, s // tc),
        in_specs=[
            tok(gw, ZX_XS), tok(gw, 0), tok(ns, ZX_B), tok(ns, ZX_C),
            pl.BlockSpec((None, tc, LANES), lambda bi, g, si: (bi, si, 0)),
            par(SSM_CONV, gw), par(1, gw),
            par(SSM_CONV, ns), par(1, ns),
            par(SSM_CONV, ns), par(1, ns),
            grp_spec, grp_spec,
            par(1, gw), par(1, gw),
        ],
        out_specs=pl.BlockSpec((None, tc, gw), lambda bi, g, si: (bi, si, g)),
        out_shape=jax.ShapeDtypeStruct((b, s, D_SSM), BF16),
        scratch_shapes=[
            pltpu.VMEM((ns, gw), F32),
            pltpu.VMEM((SUBLANES + tc, gw), F32),
            pltpu.VMEM((SUBLANES + tc, ns), F32),
            pltpu.VMEM((SUBLANES + tc, ns), F32),
            pltpu.VMEM((SSM_CHUNK, gw), F32),
        ],
        compiler_params=_params(("parallel", "parallel", "arbitrary")),
    )(zx, zx, zx, zx, dt,
      wx, bx, wb, bb, wc, bc,
      grp(dt_bias), grp(a_log), d_exp, norm_w.reshape(1, D_SSM).astype(F32))


def _rope_tables(s_len):
    pos = jnp.arange(s_len, dtype=jnp.int32)
    inv_freq = ROPE_THETA ** (-jnp.arange(ROPE_HALF, dtype=F32) / ROPE_HALF)
    ang = pos.astype(F32)[:, None] * inv_freq[None, :]
    cos, sin = jnp.cos(ang), jnp.sin(ang)
    rest = jnp.zeros((s_len, HEAD_DIM - 2 * ROPE_HALF), F32)
    z16 = jnp.zeros((s_len, ROPE_HALF), F32)
    c = jnp.concatenate([cos, cos, jnp.ones_like(rest)], axis=1)
    s1 = jnp.concatenate([-sin, z16, rest], axis=1)
    s2 = jnp.concatenate([z16, sin, rest], axis=1)
    ident = (jnp.ones_like(c), jnp.zeros_like(c), jnp.zeros_like(c))
    return tuple(jnp.stack([t, i]) for t, i in zip((c, s1, s2), ident))


def _stream_major_rows(tab, dil):
    k, s, w = tab.shape
    return tab.reshape(k, s // dil, dil, w).transpose(0, 2, 1, 3).reshape(k, s, w)


def kernel(x, p, w_in, ssm_conv_w, ssm_conv_b, ssm_dt_bias, ssm_a_log, ssm_d, ssm_norm_w, sc_conv_w,
           w_out, ln1_g, ln1_b, w_up, w_down, ln2_g, ln2_b, w_pe, w_gate, ln3_g, ln3_b):
    bsz, s_len, d_model = x.shape
    depth = w_in.shape[0]
    t = bsz * s_len
    d_ple = p.shape[-1]
    alpha = (2.0 * depth) ** 0.25
    tm = _tile(t, MM_TM)
    assert s_len % tm == 0
    rope_blocks = s_len // tm
    rope = _rope_tables(s_len)
    rope_specs = [pl.BlockSpec((None, tm, HEAD_DIM), lambda j, i: (j // 2, i % rope_blocks, 0))] * 3
    rope_by_dil = {d: tuple(_stream_major_rows(tab, d) for tab in rope) for d in ATTN_DILATIONS}
    n_grp = len(ATTN_DILATIONS)
    w_in_t = jnp.swapaxes(w_in, 1, 2)

    def flat(a):
        return a.reshape(t, a.shape[-1])

    def ln_resid(y, stats, gain, bias):
        return (flat(y), flat(stats[0]), flat(stats[1]), gain, bias)

    resid = (flat(x),)
    xb, xb4, xb16 = _layer_norm(x, ln1_g[0], ln1_b[0], normalize=False, emit_f32=False, emit_streams=True)
    for i in range(depth):
        streams = {1: xb, 4: xb4, 16: xb16}
        outs = []
        for g, dil in enumerate(ATTN_DILATIONS):
            qkv = _mmw([flat(streams[dil])], [(w_in_t, d_model, 0, lambda j, g=g: j * n_grp + g)], i,
                       3 * D_AGRP, BF16, mode="rope", extra_inputs=rope_by_dil[dil], extra_specs=rope_specs,
                       w_t=True)
            outs.append(_attn_group(qkv.reshape(bsz, dil, s_len // dil, 3 * D_AGRP)))
        y_attn = _attn_mix([o for o, _ in outs], [l for _, l in outs])

        xbf = flat(xb)
        zx = _mmw([xbf], [(w_in_t, d_model, 0, lambda j: U_Z // MM_TN + j)], i, D_ZX, F32, w_t=True)
        dt = _mmw([xbf], [(w_in_t, d_model, 0, lambda j: U_DT // LANES)], i, LANES, F32, tn=LANES, w_t=True)
        sc = _mmw([xbf], [(w_in_t, d_model, 0, lambda j: U_DT // MM_TN + j)], i, 3 * D_SC, F32,
                  w_shift=U_SC - U_DT, w_t=True)
        y_ssm = _ssd(zx.reshape(bsz, s_len, D_ZX), dt.reshape(bsz, s_len, LANES), ssm_conv_w[i], ssm_conv_b[i],
                     ssm_dt_bias[i], ssm_a_log[i], ssm_d[i], ssm_norm_w[i])
        y_sc = _short_conv(sc.reshape(bsz, s_len, 3 * D_SC), sc_conv_w[i])

        y1 = _mmw([flat(y_attn), flat(y_ssm), flat(y_sc)],
                  [(w_out, D_ATTN, 0, lambda j: j), (w_out, D_SSM, 1, lambda j: j),
                   (w_out, D_SC, (D_ATTN + D_SSM) // D_SC, lambda j: j)], i,
                  d_model, F32, mode="resid", resid=resid, alpha=alpha).reshape(bsz, s_len, d_model)
        xb, *stats = _layer_norm(y1, ln1_g[i], ln1_b[i], emit_f32=False)
        resid = ln_resid(y1, stats, ln1_g[i], ln1_b[i])

        hid = _mmw([flat(xb)], [(w_up, d_model, 0, lambda j: j)], i, w_up.shape[2], BF16, mode="relu2")
        y2 = _mmk_resid(hid, w_down, i, resid, alpha).reshape(bsz, s_len, d_model)
        xb, *stats = _layer_norm(y2, ln2_g[i], ln2_b[i], emit_f32=False)
        resid = ln_resid(y2, stats, ln2_g[i], ln2_b[i])

        y3 = _mmw([flat(xb)], [(w_gate, d_model, 0, lambda j: j)], i, d_model, F32, mode="gate",
                  resid=resid, extra_inputs=(flat(p[i]), w_pe[i]),
                  extra_specs=(pl.BlockSpec((tm, d_ple), lambda j, i_: (i_, 0)),
                               pl.BlockSpec((d_ple, MM_TN), lambda j, i_: (0, j))),
                  alpha=alpha).reshape(bsz, s_len, d_model)
        if i == depth - 1:
            out, = _layer_norm(y3, ln3_g[i], ln3_b[i], emit_bf16=False)
            return out
        xb, mu, rstd, xb4, xb16 = _layer_norm(y3, ln3_g[i], ln3_b[i], emit_f32=False, emit_streams=True)
        resid = ln_resid(y3, (mu, rstd), ln3_g[i], ln3_b[i])
```

```python
import functools

import jax
import jax.numpy as jnp
from jax import lax
from jax.experimental import pallas as pl
from jax.experimental.pallas import tpu as pltpu

HEAD_DIM = 128
ATTN_DILATIONS = (1, 4, 16)
ATTN_HPG = 4
ATTN_BLOCK = 128
ATTN_ROWS = 32
D_ATTN = ATTN_HPG * len(ATTN_DILATIONS) * HEAD_DIM
D_AGRP = ATTN_HPG * HEAD_DIM
ROPE_THETA = 500000.0
ROPE_HALF = HEAD_DIM // 8
D_SSM = 1536
SSM_HEAD_DIM = 64
SSM_HEADS = D_SSM // SSM_HEAD_DIM
SSM_GROUPS = 4
SSM_HPG = SSM_HEADS // SSM_GROUPS
SSM_GW = D_SSM // SSM_GROUPS
SSM_STATE = 128
SSM_CONV = 4
SSM_CHUNK = 128
D_SC = 1024
SC_CONV = 3
LN_EPS = 1e-5
RMS_EPS = 1e-5

U_Z = 3 * D_ATTN
U_DT = U_Z + 2 * D_SSM + 2 * SSM_GROUPS * SSM_STATE
U_SC = U_DT + SSM_HEADS
D_ZX = U_DT - U_Z
ZX_XS = D_SSM
ZX_B = 2 * D_SSM
ZX_C = ZX_B + SSM_GROUPS * SSM_STATE

LANES = 128
SUBLANES = 8
VMEM_LIMIT_BYTES = 56 * 1024 * 1024

BF16 = jnp.bfloat16
F32 = jnp.float32

MM_TM = 1024
MM_TN = 512
LN_TM = 256
LN_ROW_CHUNK = 32


def _params(sem):
    return pltpu.CompilerParams(dimension_semantics=sem, vmem_limit_bytes=VMEM_LIMIT_BYTES)


def _tile(n, pref):
    t = min(n, pref)
    while n % t:
        t -= 1
    return t


def _log2(n):
    assert n > 0 and n & (n - 1) == 0, n
    return n.bit_length() - 1


def _resid_term(r_refs, rs, alpha):
    if len(r_refs) == 1:
        return alpha * r_refs[0][rs, :]
    y_ref, mu_ref, rstd_ref, ag_ref, ab_ref = r_refs
    rep = y_ref.shape[1] // LANES
    mu = jnp.tile(mu_ref[rs, :], (1, rep))
    rstd = jnp.tile(rstd_ref[rs, :], (1, rep))
    return (y_ref[rs, :] - mu) * rstd * ag_ref[...] + ab_ref[...]


def _resid_operands(resid, alpha, tm, tn, index):
    if len(resid) == 1:
        return [resid[0]], [pl.BlockSpec((tm, tn), index)]
    y, mu, rstd, gain, bias = resid
    d = y.shape[1]

    def rows(*g):
        return (index(*g)[0], 0)

    def cols(*g):
        return (0, index(*g)[1])

    return ([y, mu, rstd, (alpha * gain).reshape(1, d), (alpha * bias).reshape(1, d)],
            [pl.BlockSpec((tm, tn), index), pl.BlockSpec((tm, LANES), rows), pl.BlockSpec((tm, LANES), rows),
             pl.BlockSpec((1, tn), cols), pl.BlockSpec((1, tn), cols)])


def _mmw_kernel(*refs, n_x, mode, alpha, w_shift, w_t, n_r):
    xs, ws = refs[:n_x], refs[n_x:2 * n_x]
    pos = 2 * n_x
    if w_shift:
        w_next_ref = refs[pos]
        pos += 1
    if mode == "rope":
        c_ref, s1_ref, s2_ref = refs[pos:pos + 3]
        pos += 3
    elif mode == "resid":
        r_refs = refs[pos:pos + n_r]
        pos += n_r
    elif mode == "gate":
        r_refs = refs[pos:pos + n_r]
        p_ref, wpe_ref = refs[pos + n_r:pos + n_r + 2]
        pos += n_r + 2
    o_ref = refs[pos]
    wbs = refs[pos + 1:pos + 1 + n_x]

    @pl.when(pl.program_id(1) == 0)
    def _():
        if w_shift:
            keep = wbs[0].shape[0] - w_shift
            wbs[0][:keep, :] = ws[0][w_shift:, :].astype(BF16)
            wbs[0][keep:, :] = w_next_ref[:w_shift, :].astype(BF16)
        else:
            for w_ref, wb_ref in zip(ws, wbs):
                wb_ref[...] = w_ref[...].astype(BF16)

    contract = (((1,), (1,)), ((), ())) if w_t else (((1,), (0,)), ((), ()))
    tm, tn = o_ref.shape
    halves = {"rope": 4, "gate": 4, "resid": 4}.get(mode, 1)
    halves = halves if tm % (halves * 2 * SUBLANES) == 0 else 1
    rows_per = tm // halves
    if mode in ("resid", "gate"):
        for hf in range(halves):
            rs = slice(hf * rows_per, (hf + 1) * rows_per)
            o_ref[rs, :] = _resid_term(r_refs, rs, alpha)

    for hf in range(halves):
        rs = slice(hf * rows_per, (hf + 1) * rows_per)
        acc = None
        for x_ref, wb_ref in zip(xs, wbs):
            part = lax.dot_general(x_ref[rs, :], wb_ref[...], contract, preferred_element_type=F32)
            acc = part if acc is None else acc + part

        if mode == "rope":
            rep = tn // HEAD_DIM
            acc = (acc * jnp.tile(c_ref[rs, :], (1, rep))
                   + pltpu.roll(acc, tn - ROPE_HALF, 1) * jnp.tile(s1_ref[rs, :], (1, rep))
                   + pltpu.roll(acc, ROPE_HALF, 1) * jnp.tile(s2_ref[rs, :], (1, rep)))
        elif mode == "relu2":
            acc = jnp.square(jnp.maximum(acc, 0.0))
        elif mode == "resid":
            acc = o_ref[rs, :] + acc
        elif mode == "gate":
            emb = jnp.dot(p_ref[rs, :].astype(BF16), wpe_ref[...].astype(BF16), preferred_element_type=F32)
            acc = o_ref[rs, :] + jax.nn.sigmoid(acc) * emb
        o_ref[rs, :] = acc.astype(o_ref.dtype)


def _mmw(xs, w_specs, layer, n, out_dtype, mode="plain", extra_inputs=(), extra_specs=(), alpha=1.0, tn=MM_TN,
         w_shift=0, w_t=False, resid=None):
    t = xs[0].shape[0]
    tm = _tile(t, MM_TM)
    assert n % tn == 0
    in_specs = [pl.BlockSpec((tm, x.shape[1]), lambda j, i: (i, 0)) for x in xs]
    w_arrays = [w for w, _, _, _ in w_specs]

    def w_spec(kdim, kblk, cfn, off=0, **kw):
        if w_t:
            return pl.BlockSpec((None, tn, kdim), lambda j, i: (layer, cfn(j) + off, kblk), **kw)
        return pl.BlockSpec((None, kdim, tn), lambda j, i: (layer, kblk, cfn(j) + off), **kw)

    for _, kdim, kblk, cfn in w_specs:
        in_specs.append(w_spec(kdim, kblk, cfn))
    if w_shift:
        assert w_t and len(w_specs) == 1 and w_shift % SUBLANES == 0
        w, kdim, kblk, cfn = w_specs[0]
        in_specs.append(w_spec(kdim, kblk, cfn, off=1, pipeline_mode=pl.Buffered(1)))
        w_arrays.append(w)
    n_r = 0
    if resid is not None:
        r_arrays, r_specs = _resid_operands(resid, alpha, tm, tn, lambda j, i: (i, j))
        n_r = len(r_arrays)
        extra_inputs = tuple(r_arrays) + tuple(extra_inputs)
        extra_specs = tuple(r_specs) + tuple(extra_specs)
    return pl.pallas_call(
        functools.partial(_mmw_kernel, n_x=len(xs), mode=mode, alpha=alpha, w_shift=w_shift, w_t=w_t, n_r=n_r),
        grid=(n // tn, t // tm),
        in_specs=in_specs + list(extra_specs),
        out_specs=pl.BlockSpec((tm, tn), lambda j, i: (i, j)),
        out_shape=jax.ShapeDtypeStruct((t, n), out_dtype),
        scratch_shapes=[pltpu.VMEM((tn, kdim) if w_t else (kdim, tn), BF16) for _, kdim, _, _ in w_specs],
        compiler_params=_params(("arbitrary", "arbitrary")),
    )(*xs, *w_arrays, *extra_inputs)


def _mmk_kernel(x_ref, w_ref, *refs, alpha, row_chunk):
    r_refs, o_ref, wb_ref = refs[:-2], refs[-2], refs[-1]
    k = pl.program_id(2)

    chunks = [slice(c * row_chunk, (c + 1) * row_chunk) for c in range(o_ref.shape[0] // row_chunk)]

    @pl.when(k == 0)
    def _():
        for rs in chunks:
            o_ref[rs, :] = _resid_term(r_refs, rs, alpha)

    wb_ref[...] = w_ref[...].astype(BF16)
    for rs in chunks:
        o_ref[rs, :] += jnp.dot(x_ref[rs, :], wb_ref[...], preferred_element_type=F32)


def _mmk_resid(x, w, layer, resid, alpha):
    t, kdim = x.shape
    n = w.shape[2]
    tm, tn, tk = _tile(t, 2048), _tile(n, 1024), _tile(kdim, 1024)
    r_arrays, r_specs = _resid_operands(resid, alpha, tm, tn, lambda i, j, k: (i, j))
    return pl.pallas_call(
        functools.partial(_mmk_kernel, alpha=alpha, row_chunk=_tile(tm, 512)),
        grid=(t // tm, n // tn, kdim // tk),
        in_specs=[pl.BlockSpec((tm, tk), lambda i, j, k: (i, k)),
                  pl.BlockSpec((None, tk, tn), lambda i, j, k: (layer, k, j)),
                  *r_specs],
        out_specs=pl.BlockSpec((tm, tn), lambda i, j, k: (i, j)),
        out_shape=jax.ShapeDtypeStruct((t, n), F32),
        scratch_shapes=[pltpu.VMEM((tk, tn), BF16)],
        compiler_params=_params(("parallel", "parallel", "arbitrary")),
    )(x, w, *r_arrays)


def _stream_perm(tm, dil, inverse):
    n = tm // dil
    a = lax.broadcasted_iota(jnp.int32, (tm, tm), 0)
    b = lax.broadcasted_iota(jnp.int32, (tm, tm), 1)
    if inverse:
        src = jnp.bitwise_and(a, dil - 1) * n + lax.shift_right_logical(a, _log2(dil))
    else:
        src = jnp.bitwise_and(a, n - 1) * dil + lax.shift_right_logical(a, _log2(n))
    return b == src


def _ln_kernel(*refs, normalize, emit_f32, emit_bf16, emit_streams):
    y_ref, g_ref, b_ref = refs[:3]
    outs = list(refs[3:])
    f32_ref = outs.pop(0) if emit_f32 else None
    bf16_ref = outs.pop(0) if emit_bf16 else None
    emit_stats = normalize and not emit_f32
    if emit_stats:
        mu_ref, rstd_ref = outs.pop(0), outs.pop(0)
    assert emit_bf16 or not emit_streams
    tm = y_ref.shape[0]
    rows = _tile(tm, LN_ROW_CHUNK)

    def norm_rows(ri, carry):
        rs = pl.ds(pl.multiple_of(ri * rows, rows), rows)
        y = y_ref[rs, :]
        if normalize:
            mu = jnp.mean(y, axis=-1, keepdims=True)
            yc = y - mu
            var = jnp.mean(jnp.square(yc), axis=-1, keepdims=True)
            rstd = lax.rsqrt(var + LN_EPS)
            y = yc * rstd * g_ref[...] + b_ref[...]
            if emit_stats:
                mu_ref[rs, :] = jnp.broadcast_to(mu, (rows, LANES))
                rstd_ref[rs, :] = jnp.broadcast_to(rstd, (rows, LANES))
        if emit_f32:
            f32_ref[rs, :] = y
        if emit_bf16:
            bf16_ref[rs, :] = y.astype(BF16)
        return carry

    lax.fori_loop(0, tm // rows, norm_rows, 0)

    if emit_streams:
        y16 = bf16_ref[...]
        for dil in ATTN_DILATIONS[1:]:
            o_ref = outs.pop(0)
            perm = jnp.where(_stream_perm(tm, dil, inverse=False), 1.0, 0.0).astype(BF16)
            moved = jnp.dot(perm, y16, preferred_element_type=F32).astype(BF16)
            n = tm // dil
            for r in range(dil):
                o_ref[r] = moved[r * n:(r + 1) * n]


def _layer_norm(y3, gain, bias, normalize=True, emit_f32=True, emit_bf16=True, emit_streams=False):
    b, s, d = y3.shape
    tm = _tile(s, LN_TM if emit_streams else 2 * LN_TM)
    tok = pl.BlockSpec((None, tm, d), lambda bi, i: (bi, i, 0))
    vec = pl.BlockSpec((1, d), lambda bi, i: (0, 0))
    out_specs, out_shape = [], []
    if emit_f32:
        out_specs.append(tok)
        out_shape.append(jax.ShapeDtypeStruct((b, s, d), F32))
    if emit_bf16:
        out_specs.append(tok)
        out_shape.append(jax.ShapeDtypeStruct((b, s, d), BF16))
    if normalize and not emit_f32:
        for _ in range(2):
            out_specs.append(pl.BlockSpec((None, tm, LANES), lambda bi, i: (bi, i, 0)))
            out_shape.append(jax.ShapeDtypeStruct((b, s, LANES), F32))
    if emit_streams:
        for dil in ATTN_DILATIONS[1:]:
            out_specs.append(pl.BlockSpec((None, dil, tm // dil, d), lambda bi, i: (bi, 0, i, 0)))
            out_shape.append(jax.ShapeDtypeStruct((b, dil, s // dil, d), BF16))
    return pl.pallas_call(
        functools.partial(_ln_kernel, normalize=normalize, emit_f32=emit_f32, emit_bf16=emit_bf16,
                          emit_streams=emit_streams),
        grid=(b, s // tm),
        in_specs=[tok, vec, vec],
        out_specs=out_specs,
        out_shape=out_shape,
        compiler_params=_params(("parallel", "parallel")),
    )(y3, gain.reshape(1, d), bias.reshape(1, d))


def _attn_kernel(q_ref, kc_ref, kp_ref, vc_ref, vp_ref, o_ref, l_ref, s_scr, p_scr, m_scr, bias_scr, *, tq, scale):
    n = pl.program_id(2)
    blk = ATTN_BLOCK
    units = [(h, i) for h in range(ATTN_HPG) for i in range(tq // blk)]
    nt = (((1,), (1,)), ((), ()))

    def keys_vals(ref_c, ref_p, h, i):
        hs = slice(h * HEAD_DIM, (h + 1) * HEAD_DIM)
        if i == 0:
            return jnp.concatenate([ref_p[:, hs], ref_c[:blk, hs]], axis=0)
        return ref_c[(i - 1) * blk:(i + 1) * blk, hs]

    row = lax.broadcasted_iota(jnp.int32, (blk, 2 * blk), 0)
    col = lax.broadcasted_iota(jnp.int32, (blk, 2 * blk), 1)
    bias_scr[...] = jnp.where(col < blk,
                              jnp.where(col >= row, 0.0, -jnp.inf),
                              jnp.where(col - blk <= row, 0.0, -jnp.inf))
    col_r = lax.broadcasted_iota(jnp.int32, (ATTN_ROWS, 2 * blk), 1)
    first_pen = jnp.where(col_r < blk, jnp.where(n == 0, -jnp.inf, 0.0), 0.0)

    for u, (h, i) in enumerate(units):
        q = q_ref[i * blk:(i + 1) * blk, h * HEAD_DIM:(h + 1) * HEAD_DIM]
        s_scr[u] = lax.dot_general(q, keys_vals(kc_ref, kp_ref, h, i), nt, preferred_element_type=F32)

    for u, (h, i) in enumerate(units):
        for r in range(blk // ATTN_ROWS):
            rs = slice(r * ATTN_ROWS, (r + 1) * ATTN_ROWS)
            s = s_scr[u, rs, :] * scale + bias_scr[rs, :]
            if i == 0:
                s = s + first_pen
            m = jnp.max(s, axis=-1, keepdims=True)
            p_scr[u, rs, :] = jnp.exp(s - m).astype(BF16)
            m_scr[u, rs, :] = jnp.broadcast_to(m, (ATTN_ROWS, HEAD_DIM))

    ones = jnp.ones((2 * blk, HEAD_DIM), BF16)
    for u, (h, i) in enumerate(units):
        v_aug = jnp.concatenate([keys_vals(vc_ref, vp_ref, h, i), ones], axis=1)
        pv = jnp.dot(p_scr[u], v_aug, preferred_element_type=F32)
        den = pv[:, HEAD_DIM:]
        rs, hs = slice(i * blk, (i + 1) * blk), slice(h * HEAD_DIM, (h + 1) * HEAD_DIM)
        o_ref[rs, hs] = pv[:, :HEAD_DIM] / den
        l_ref[rs, hs] = m_scr[u] + jnp.log(den)


def _attn_group(qkv):
    b, dil, sub, _ = qkv.shape
    tq = _tile(sub, 512)
    per = tq // ATTN_BLOCK
    n_units = ATTN_HPG * per

    def cur(c):
        return pl.BlockSpec((None, None, tq, D_AGRP), lambda bi, r, n: (bi, r, n, c))

    def prev(c):
        return pl.BlockSpec((None, None, ATTN_BLOCK, D_AGRP),
                            lambda bi, r, n: (bi, r, jnp.maximum(n * per - 1, 0), c))

    out_sds = jax.ShapeDtypeStruct((b, dil, sub, D_AGRP), F32)
    return pl.pallas_call(
        functools.partial(_attn_kernel, tq=tq, scale=HEAD_DIM ** -0.5),
        grid=(b, dil, sub // tq),
        in_specs=[cur(0), cur(1), prev(1), cur(2), prev(2)],
        out_specs=[cur(0), cur(0)],
        out_shape=[out_sds, out_sds],
        scratch_shapes=[pltpu.VMEM((n_units, ATTN_BLOCK, 2 * ATTN_BLOCK), F32),
                        pltpu.VMEM((n_units, ATTN_BLOCK, 2 * ATTN_BLOCK), BF16),
                        pltpu.VMEM((n_units, ATTN_BLOCK, HEAD_DIM), F32),
                        pltpu.VMEM((ATTN_BLOCK, 2 * ATTN_BLOCK), F32)],
        compiler_params=_params(("parallel", "parallel", "arbitrary")),
    )(qkv, qkv, qkv, qkv, qkv)


def _to_token_order(x_ref, dil):
    if dil == 1:
        return x_ref[0]
    n, w = x_ref.shape[1], x_ref.shape[2]
    x = x_ref[...].reshape(dil * n, w)
    perm = jnp.where(_stream_perm(dil * n, dil, inverse=True), 1.0, 0.0).astype(BF16)
    hi = x.astype(BF16)
    r1 = x - hi.astype(F32)
    mid = r1.astype(BF16)
    lo = (r1 - mid.astype(F32)).astype(BF16)
    return (jnp.dot(perm, hi, preferred_element_type=F32)
            + jnp.dot(perm, mid, preferred_element_type=F32)
            + jnp.dot(perm, lo, preferred_element_type=F32))


def _attn_mix_kernel(o0, o1, o2, l0, l1, l2, y_ref):
    os_ = [_to_token_order(o, d) for o, d in zip((o0, o1, o2), ATTN_DILATIONS)]
    ls = [_to_token_order(l, d) for l, d in zip((l0, l1, l2), ATTN_DILATIONS)]
    m = jnp.maximum(jnp.maximum(ls[0], ls[1]), ls[2])
    es = [jnp.exp(l - m) for l in ls]
    inv = 1.0 / (es[0] + es[1] + es[2])
    for g in range(len(ATTN_DILATIONS)):
        y_ref[:, g * D_AGRP:(g + 1) * D_AGRP] = (os_[g] * (es[g] * inv)).astype(y_ref.dtype)


def _attn_mix(os_, ls_):
    b, _, s, _ = os_[0].shape
    tm = _tile(s, LN_TM)
    specs = [pl.BlockSpec((None, d, tm // d, D_AGRP), lambda bi, i: (bi, 0, i, 0)) for d in ATTN_DILATIONS]
    return pl.pallas_call(
        _attn_mix_kernel,
        grid=(b, s // tm),
        in_specs=specs + specs,
        out_specs=pl.BlockSpec((None, tm, D_ATTN), lambda bi, i: (bi, i, 0)),
        out_shape=jax.ShapeDtypeStruct((b, s, D_ATTN), BF16),
        compiler_params=_params(("parallel", "parallel")),
    )(*os_, *ls_)


def _shift_rows(cur, halo, sh):
    rolled = pltpu.roll(cur, sh, 0)
    hr = pltpu.roll(halo, sh, 0)
    row = lax.broadcasted_iota(jnp.int32, halo.shape, 0)
    first = jnp.where(row < sh, hr, rolled[:SUBLANES])
    return jnp.concatenate([first, rolled[SUBLANES:]], axis=0)


def _causal_conv(cur, halo, w):
    kk = w.shape[0]
    acc = w[kk - 1:kk] * cur
    for sh in range(1, kk):
        acc = acc + w[kk - 1 - sh:kk - sh] * _shift_rows(cur, halo, sh)
    return acc


def _sc_kernel(b_ref, c_ref, h_ref, ch_ref, hh_ref, w_ref, o_ref):
    s = pl.program_id(1)
    g = c_ref[...] * h_ref[...]
    gh = ch_ref[...] * hh_ref[...] * jnp.where(s > 0, 1.0, 0.0)
    o_ref[...] = (b_ref[...] * _causal_conv(g, gh, w_ref[...])).astype(o_ref.dtype)


def _short_conv(sc, conv_w):
    b, s, _ = sc.shape
    ts = _tile(s, 512)
    per = ts // SUBLANES

    def cur(j):
        return pl.BlockSpec((None, ts, D_SC), lambda bi, si: (bi, si, j))

    def halo(j):
        return pl.BlockSpec((None, SUBLANES, D_SC), lambda bi, si: (bi, jnp.maximum(si * per - 1, 0), j))

    return pl.pallas_call(
        _sc_kernel,
        grid=(b, s // ts),
        in_specs=[cur(0), cur(1), cur(2), halo(1), halo(2),
                  pl.BlockSpec((SC_CONV, D_SC), lambda bi, si: (0, 0))],
        out_specs=pl.BlockSpec((None, ts, D_SC), lambda bi, si: (bi, si, 0)),
        out_shape=jax.ShapeDtypeStruct((b, s, D_SC), BF16),
        compiler_params=_params(("parallel", "arbitrary")),
    )(sc, sc, sc, sc, sc, conv_w)


def _ssd_kernel(xs_ref, z_ref, b_ref, c_ref, dt_ref,
                wx_ref, bx_ref, wb_ref, bb_ref, wc_ref, bc_ref,
                dtb_ref, alog_ref, dsk_ref, nw_ref,
                o_ref,
                h_ref, xp_ref, bp_ref, cp_ref, y_ref, *, tc):
    g = pl.program_id(1)
    s = pl.program_id(2)
    q = SSM_CHUNK
    p = SSM_HEAD_DIM
    pads = ((xs_ref, xp_ref, wx_ref, bx_ref), (b_ref, bp_ref, wb_ref, bb_ref), (c_ref, cp_ref, wc_ref, bc_ref))

    @pl.when(s == 0)
    def _():
        h_ref[...] = jnp.zeros_like(h_ref)
        for _, pad_ref, _, _ in pads:
            pad_ref[:SUBLANES, :] = jnp.zeros((SUBLANES, pad_ref.shape[1]), F32)

    for raw_ref, pad_ref, _, _ in pads:
        pad_ref[SUBLANES:, :] = raw_ref[...]

    dt_shift = (LANES - g * SSM_HPG) % LANES
    a_row = -jnp.exp(alog_ref[...])
    dskip = dsk_ref[...]
    li = lax.broadcasted_iota(jnp.int32, (q, q), 0)
    si = lax.broadcasted_iota(jnp.int32, (q, q), 1)
    causal = li >= si
    tri = jnp.where(causal, 1.0, 0.0).astype(F32)
    first_head = lax.broadcasted_iota(jnp.int32, (q, 2 * p), 1) < p
    nt = (((1,), (1,)), ((), ()))

    def conv_silu(pad_ref, w_ref, bias_ref, r0):
        w = w_ref[...]
        win = pad_ref[pl.ds(r0, q + SUBLANES), :]
        acc = bias_ref[...]
        for k in range(SSM_CONV):
            lo = SUBLANES - (SSM_CONV - 1) + k
            acc = acc + w[k:k + 1] * win[lo:lo + q]
        return acc * jax.nn.sigmoid(acc)

    def chunk(ci, carry):
        r0 = pl.multiple_of(ci * q, q)
        xq = conv_silu(xp_ref, wx_ref, bx_ref, r0)
        bq = conv_silu(bp_ref, wb_ref, bb_ref, r0)
        cq = conv_silu(cp_ref, wc_ref, bc_ref, r0)
        dt_raw = pltpu.roll(dt_ref[pl.ds(r0, q), :], dt_shift, 1)
        dtq = jax.nn.softplus(dt_raw + dtb_ref[...])
        cs = jnp.dot(tri, dtq * a_row, precision=lax.Precision.HIGHEST,
                     preferred_element_type=F32)
        cs_t = cs.T
        dt_t = dtq.T
        bb = bq.astype(BF16)
        cb16 = cq.astype(BF16)
        cb = lax.dot_general(cb16, bb, nt, preferred_element_type=F32)
        b_t = bq.T
        hprev = h_ref[...]
        y_off = jnp.dot(cb16, hprev.astype(BF16), preferred_element_type=F32)
        for jp in range(SSM_HPG // 2):
            cols = slice(jp * 2 * p, (jp + 1) * 2 * p)
            x2 = xq[:, cols]
            y_d, st, e_col, e_last = None, None, [], []
            for half in range(2):
                j = 2 * jp + half
                xh = (jnp.where(first_head, x2, 0.0) if half == 0 else jnp.where(first_head, 0.0, x2)).astype(BF16)
                col = cs[:, j:j + 1]
                row = cs_t[j:j + 1, :]
                dtrow = dt_t[j:j + 1, :]
                decay = jnp.exp(jnp.where(causal, col - row, -jnp.inf))
                m = (cb * decay * dtrow).astype(BF16)
                part = jnp.dot(m, xh, preferred_element_type=F32)
                y_d = part if y_d is None else y_d + part
                last = cs[q - 1:q, j:j + 1]
                w_row = jnp.exp(last - row) * dtrow
                part = jnp.dot((b_t * w_row).astype(BF16), xh, preferred_element_type=F32)
                st = part if st is None else st + part
                e_col.append(jnp.exp(col))
                e_last.append(jnp.exp(last))
            y_ref[:, cols] = (y_d + y_off[:, cols] * jnp.where(first_head, e_col[0], e_col[1])
                              + dskip[:, cols] * x2)
            h_ref[:, cols] = hprev[:, cols] * jnp.where(first_head[:1], e_last[0], e_last[1]) + st

        z = z_ref[pl.ds(r0, q), :]
        yv = y_ref[...] * (z * jax.nn.sigmoid(z))
        ms = jnp.mean(jnp.square(yv), axis=-1, keepdims=True)
        o_ref[pl.ds(r0, q), :] = (yv * lax.rsqrt(ms + RMS_EPS) * nw_ref[...]).astype(o_ref.dtype)
        return carry

    lax.fori_loop(0, tc // q, chunk, 0)

    for _, pad_ref, _, _ in pads:
        pad_ref[:SUBLANES, :] = pad_ref[tc:, :]


def _ssd(zx, dt, conv_w, conv_b, dt_bias, a_log, d_skip, norm_w):
    b, s, _ = zx.shape
    tc = _tile(s, 2048)
    gw, ns, hpg = SSM_GW, SSM_STATE, SSM_HPG

    def tok(width, base):
        return pl.BlockSpec((None, tc, width), lambda bi, g, si: (bi, si, base // width + g))

    def par(rows, width):
        return pl.BlockSpec((rows, width), lambda bi, g, si: (0, g))

    def grp(arr):
        a = arr.reshape(SSM_GROUPS, 1, hpg).astype(F32)
        return jnp.pad(a, ((0, 0), (0, 0), (0, LANES - hpg)))

    grp_spec = pl.BlockSpec((None, 1, LANES), lambda bi, g, si: (g, 0, 0))
    wx, wb, wc = conv_w[:, :D_SSM], conv_w[:, D_SSM:D_SSM + 4 * ns], conv_w[:, D_SSM + 4 * ns:]
    cb2 = conv_b.reshape(1, -1)
    bx, bb, bc = cb2[:, :D_SSM], cb2[:, D_SSM:D_SSM + 4 * ns], cb2[:, D_SSM + 4 * ns:]
    d_exp = jnp.repeat(d_skip.astype(F32), SSM_HEAD_DIM).reshape(1, D_SSM)
    return pl.pallas_call(
        functools.partial(_ssd_kernel, tc=tc),
        grid=(b, SSM_GROUPS, s // tc),
        in_specs=[
            tok(gw, ZX_XS), tok(gw, 0), tok(ns, ZX_B), tok(ns, ZX_C),
            pl.BlockSpec((None, tc, LANES), lambda bi, g, si: (bi, si, 0)),
            par(SSM_CONV, gw), par(1, gw),
            par(SSM_CONV, ns), par(1, ns),
            par(SSM_CONV, ns), par(1, ns),
            grp_spec, grp_spec,
            par(1, gw), par(1, gw),
        ],
        out_specs=pl.BlockSpec((None, tc, gw), lambda bi, g, si: (bi, si, g)),
        out_shape=jax.ShapeDtypeStruct((b, s, D_SSM), BF16),
        scratch_shapes=[
            pltpu.VMEM((ns, gw), F32),
            pltpu.VMEM((SUBLANES + tc, gw), F32),
            pltpu.VMEM((SUBLANES + tc, ns), F32),
            pltpu.VMEM((SUBLANES + tc, ns), F32),
            pltpu.VMEM((SSM_CHUNK, gw), F32),
        ],
        compiler_params=_params(("parallel", "parallel", "arbitrary")),
    )(zx, zx, zx, zx, dt,
      wx, bx, wb, bb, wc, bc,
      grp(dt_bias), grp(a_log), d_exp, norm_w.reshape(1, D_SSM).astype(F32))


def _rope_tables(s_len):
    pos = jnp.arange(s_len, dtype=jnp.int32)
    inv_freq = ROPE_THETA ** (-jnp.arange(ROPE_HALF, dtype=F32) / ROPE_HALF)
    ang = pos.astype(F32)[:, None] * inv_freq[None, :]
    cos, sin = jnp.cos(ang), jnp.sin(ang)
    rest = jnp.zeros((s_len, HEAD_DIM - 2 * ROPE_HALF), F32)
    z16 = jnp.zeros((s_len, ROPE_HALF), F32)
    c = jnp.concatenate([cos, cos, jnp.ones_like(rest)], axis=1)
    s1 = jnp.concatenate([-sin, z16, rest], axis=1)
    s2 = jnp.concatenate([z16, sin, rest], axis=1)
    ident = (jnp.ones_like(c), jnp.zeros_like(c), jnp.zeros_like(c))
    return tuple(jnp.stack([t, i]) for t, i in zip((c, s1, s2), ident))


def _stream_major_rows(tab, dil):
    k, s, w = tab.shape
    return tab.reshape(k, s // dil, dil, w).transpose(0, 2, 1, 3).reshape(k, s, w)


def kernel(x, p, w_in, ssm_conv_w, ssm_conv_b, ssm_dt_bias, ssm_a_log, ssm_d, ssm_norm_w, sc_conv_w,
           w_out, ln1_g, ln1_b, w_up, w_down, ln2_g, ln2_b, w_pe, w_gate, ln3_g, ln3_b):
    bsz, s_len, d_model = x.shape
    depth = w_in.shape[0]
    t = bsz * s_len
    d_ple = p.shape[-1]
    alpha = (2.0 * depth) ** 0.25
    tm = _tile(t, MM_TM)
    assert s_len % tm == 0
    rope_blocks = s_len // tm
    rope = _rope_tables(s_len)
    rope_specs = [pl.BlockSpec((None, tm, HEAD_DIM), lambda j, i: (j // 2, i % rope_blocks, 0))] * 3
    rope_by_dil = {d: tuple(_stream_major_rows(tab, d) for tab in rope) for d in ATTN_DILATIONS}
    n_grp = len(ATTN_DILATIONS)
    w_in_t = jnp.swapaxes(w_in, 1, 2)

    def flat(a):
        return a.reshape(t, a.shape[-1])

    def ln_resid(y, stats, gain, bias):
        return (flat(y), flat(stats[0]), flat(stats[1]), gain, bias)

    resid = (flat(x),)
    xb, xb4, xb16 = _layer_norm(x, ln1_g[0], ln1_b[0], normalize=False, emit_f32=False, emit_streams=True)
    for i in range(depth):
        streams = {1: xb, 4: xb4, 16: xb16}
        outs = []
        for g, dil in enumerate(ATTN_DILATIONS):
            qkv = _mmw([flat(streams[dil])], [(w_in_t, d_model, 0, lambda j, g=g: j * n_grp + g)], i,
                       3 * D_AGRP, BF16, mode="rope", extra_inputs=rope_by_dil[dil], extra_specs=rope_specs,
                       w_t=True)
            outs.append(_attn_group(qkv.reshape(bsz, dil, s_len // dil, 3 * D_AGRP)))
        y_attn = _attn_mix([o for o, _ in outs], [l for _, l in outs])

        xbf = flat(xb)
        zx = _mmw([xbf], [(w_in_t, d_model, 0, lambda j: U_Z // MM_TN + j)], i, D_ZX, F32, w_t=True)
        dt = _mmw([xbf], [(w_in_t, d_model, 0, lambda j: U_DT // LANES)], i, LANES, F32, tn=LANES, w_t=True)
        sc = _mmw([xbf], [(w_in_t, d_model, 0, lambda j: U_DT // MM_TN + j)], i, 3 * D_SC, F32,
                  w_shift=U_SC - U_DT, w_t=True)
        y_ssm = _ssd(zx.reshape(bsz, s_len, D_ZX), dt.reshape(bsz, s_len, LANES), ssm_conv_w[i], ssm_conv_b[i],
                     ssm_dt_bias[i], ssm_a_log[i], ssm_d[i], ssm_norm_w[i])
        y_sc = _short_conv(sc.reshape(bsz, s_len, 3 * D_SC), sc_conv_w[i])

        y1 = _mmw([flat(y_attn), flat(y_ssm), flat(y_sc)],
                  [(w_out, D_ATTN, 0, lambda j: j), (w_out, D_SSM, 1, lambda j: j),
                   (w_out, D_SC, (D_ATTN + D_SSM) // D_SC, lambda j: j)], i,
                  d_model, F32, mode="resid", resid=resid, alpha=alpha).reshape(bsz, s_len, d_model)
        xb, *stats = _layer_norm(y1, ln1_g[i], ln1_b[i], emit_f32=False)
        resid = ln_resid(y1, stats, ln1_g[i], ln1_b[i])

        hid = _mmw([flat(xb)], [(w_up, d_model, 0, lambda j: j)], i, w_up.shape[2], BF16, mode="relu2")
        y2 = _mmk_resid(hid, w_down, i, resid, alpha).reshape(bsz, s_len, d_model)
        xb, *stats = _layer_norm(y2, ln2_g[i], ln2_b[i], emit_f32=False)
        resid = ln_resid(y2, stats, ln2_g[i], ln2_b[i])

        y3 = _mmw([flat(xb)], [(w_gate, d_model, 0, lambda j: j)], i, d_model, F32, mode="gate",
                  resid=resid, extra_inputs=(flat(p[i]), w_pe[i]),
                  extra_specs=(pl.BlockSpec((tm, d_ple), lambda j, i_: (i_, 0)),
                               pl.BlockSpec((d_ple, MM_TN), lambda j, i_: (0, j))),
                  alpha=alpha).reshape(bsz, s_len, d_model)
        if i == depth - 1:
            out, = _layer_norm(y3, ln3_g[i], ln3_b[i], emit_bf16=False)
            return out
        xb, mu, rstd, xb4, xb16 = _layer_norm(y3, ln3_g[i], ln3_b[i], emit_f32=False, emit_streams=True)
        resid = ln_resid(y3, (mu, rstd), ln3_g[i], ln3_b[i])
```

```python
import functools

import jax
import jax.numpy as jnp
from jax import lax
from jax.experimental import pallas as pl
from jax.experimental.pallas import tpu as pltpu

HEAD_DIM = 128
ATTN_DILATIONS = (1, 4, 16)
ATTN_HPG = 4
ATTN_BLOCK = 128
ATTN_ROWS = 32
D_ATTN = ATTN_HPG * len(ATTN_DILATIONS) * HEAD_DIM
D_AGRP = ATTN_HPG * HEAD_DIM
ROPE_THETA = 500000.0
ROPE_HALF = HEAD_DIM // 8
D_SSM = 1536
SSM_HEAD_DIM = 64
SSM_HEADS = D_SSM // SSM_HEAD_DIM
SSM_GROUPS = 4
SSM_HPG = SSM_HEADS // SSM_GROUPS
SSM_GW = D_SSM // SSM_GROUPS
SSM_STATE = 128
SSM_CONV = 4
SSM_CHUNK = 128
SSM_UNROLL = 2
D_SC = 1024
SC_CONV = 3
LN_EPS = 1e-5
RMS_EPS = 1e-5

U_Z = 3 * D_ATTN
U_DT = U_Z + 2 * D_SSM + 2 * SSM_GROUPS * SSM_STATE
U_SC = U_DT + SSM_HEADS
D_ZX = U_DT - U_Z
ZX_XS = D_SSM
ZX_B = 2 * D_SSM
ZX_C = ZX_B + SSM_GROUPS * SSM_STATE

LANES = 128
SUBLANES = 8
VMEM_LIMIT_BYTES = 56 * 1024 * 1024

BF16 = jnp.bfloat16
F32 = jnp.float32

MM_TM = 1024
MM_TN = 512
LN_TM = 512
LN_ROW_CHUNK = 8
LN_UNROLL = 4
PERM_ROWS = 256


def _params(sem):
    return pltpu.CompilerParams(dimension_semantics=sem, vmem_limit_bytes=VMEM_LIMIT_BYTES)


def _tile(n, pref):
    t = min(n, pref)
    while n % t:
        t -= 1
    return t


def _log2(n):
    assert n > 0 and n & (n - 1) == 0, n
    return n.bit_length() - 1


def _resid_term(r_refs, rs, alpha):
    if len(r_refs) == 1:
        return alpha * r_refs[0][rs, :]
    y_ref, mu_ref, rstd_ref, ag_ref, ab_ref = r_refs
    rep = y_ref.shape[1] // LANES
    mu = jnp.tile(mu_ref[rs, :], (1, rep))
    rstd = jnp.tile(rstd_ref[rs, :], (1, rep))
    return (y_ref[rs, :] - mu) * rstd * ag_ref[...] + ab_ref[...]


def _resid_operands(resid, alpha, tm, tn, index):
    if len(resid) == 1:
        return [resid[0]], [pl.BlockSpec((tm, tn), index)]
    y, mu, rstd, gain, bias = resid
    d = y.shape[1]

    def rows(*g):
        return (index(*g)[0], 0)

    def cols(*g):
        return (0, index(*g)[1])

    return ([y, mu, rstd, (alpha * gain).reshape(1, d), (alpha * bias).reshape(1, d)],
            [pl.BlockSpec((tm, tn), index), pl.BlockSpec((tm, LANES), rows), pl.BlockSpec((tm, LANES), rows),
             pl.BlockSpec((1, tn), cols), pl.BlockSpec((1, tn), cols)])


def _mmw_kernel(*refs, n_x, mode, alpha, w_shift, w_t, n_r, dil):
    xs, ws = refs[:n_x], refs[n_x:2 * n_x]
    pos = 2 * n_x
    if w_shift:
        w_next_ref = refs[pos]
        pos += 1
    if mode == "rope":
        c_ref, s1_ref, s2_ref = refs[pos:pos + 3]
        pos += 3
    elif mode == "resid":
        r_refs = refs[pos:pos + n_r]
        pos += n_r
    elif mode == "gate":
        r_refs = refs[pos:pos + n_r]
        p_ref, wpe_ref = refs[pos + n_r:pos + n_r + 2]
        pos += n_r + 2
    o_ref = refs[pos]
    wbs = refs[pos + 1:pos + 1 + n_x]
    stage_ref = refs[pos + 1 + n_x] if dil > 1 else None

    @pl.when(pl.program_id(1) == 0)
    def _():
        if w_shift:
            keep = wbs[0].shape[0] - w_shift
            wbs[0][:keep, :] = ws[0][w_shift:, :].astype(BF16)
            wbs[0][keep:, :] = w_next_ref[:w_shift, :].astype(BF16)
        else:
            for w_ref, wb_ref in zip(ws, wbs):
                wb_ref[...] = w_ref[...].astype(BF16)

    contract = (((1,), (1,)), ((), ())) if w_t else (((1,), (0,)), ((), ()))
    tm, tn = xs[0].shape[0], o_ref.shape[-1]
    halves = {"rope": 4, "gate": 4, "resid": 4}.get(mode, 1)
    halves = halves if tm % (halves * 2 * SUBLANES) == 0 else 1
    rows_per = tm // halves
    if dil > 1:
        rows_per = min(tm, PERM_ROWS)
        halves = tm // rows_per
        perm = jnp.where(_stream_perm(rows_per, dil, inverse=False), 1.0, 0.0).astype(BF16)
    if mode in ("resid", "gate"):
        for hf in range(halves):
            rs = slice(hf * rows_per, (hf + 1) * rows_per)
            o_ref[rs, :] = _resid_term(r_refs, rs, alpha)

    for hf in range(halves):
        rs = slice(hf * rows_per, (hf + 1) * rows_per)
        acc = None
        for x_ref, wb_ref in zip(xs, wbs):
            part = lax.dot_general(x_ref[rs, :], wb_ref[...], contract, preferred_element_type=F32)
            acc = part if acc is None else acc + part

        if mode == "rope":
            rep = tn // HEAD_DIM
            acc = (acc * jnp.tile(c_ref[rs, :], (1, rep))
                   + pltpu.roll(acc, tn - ROPE_HALF, 1) * jnp.tile(s1_ref[rs, :], (1, rep))
                   + pltpu.roll(acc, ROPE_HALF, 1) * jnp.tile(s2_ref[rs, :], (1, rep)))
        elif mode == "relu2":
            acc = jnp.square(jnp.maximum(acc, 0.0))
        elif mode == "resid":
            acc = o_ref[rs, :] + acc
        elif mode == "gate":
            emb = jnp.dot(p_ref[rs, :].astype(BF16), wpe_ref[...].astype(BF16), preferred_element_type=F32)
            acc = o_ref[rs, :] + jax.nn.sigmoid(acc) * emb
        (stage_ref if dil > 1 else o_ref)[rs, :] = acc.astype(o_ref.dtype)

    if dil > 1:
        n = rows_per // dil
        for hf in range(halves):
            rs = slice(hf * rows_per, (hf + 1) * rows_per)
            moved = jnp.dot(perm, stage_ref[rs, :], preferred_element_type=F32).astype(o_ref.dtype)
            for r in range(dil):
                o_ref[r, hf * n:(hf + 1) * n, :] = moved[r * n:(r + 1) * n]


def _mmw(xs, w_specs, layer, n, out_dtype, mode="plain", extra_inputs=(), extra_specs=(), alpha=1.0, tn=MM_TN,
         w_shift=0, w_t=False, resid=None, stream=None):
    t = xs[0].shape[0]
    tm = _tile(t, MM_TM)
    assert n % tn == 0
    dil = 1
    out_spec = pl.BlockSpec((tm, tn), lambda j, i: (i, j))
    out_shape = jax.ShapeDtypeStruct((t, n), out_dtype)
    if stream is not None and stream[1] > 1:
        batch, dil = stream
        per_seq = t // batch // tm
        assert tm % PERM_ROWS == 0 or tm < PERM_ROWS
        out_spec = pl.BlockSpec((None, dil, tm // dil, tn), lambda j, i: (i // per_seq, 0, i % per_seq, j))
        out_shape = jax.ShapeDtypeStruct((batch, dil, t // batch // dil, n), out_dtype)
    in_specs = [pl.BlockSpec((tm, x.shape[1]), lambda j, i: (i, 0)) for x in xs]
    w_arrays = [w for w, _, _, _ in w_specs]

    def w_spec(kdim, kblk, cfn, off=0, **kw):
        if w_t:
            return pl.BlockSpec((None, tn, kdim), lambda j, i: (layer, cfn(j) + off, kblk), **kw)
        return pl.BlockSpec((None, kdim, tn), lambda j, i: (layer, kblk, cfn(j) + off), **kw)

    for _, kdim, kblk, cfn in w_specs:
        in_specs.append(w_spec(kdim, kblk, cfn))
    if w_shift:
        assert w_t and len(w_specs) == 1 and w_shift % SUBLANES == 0
        w, kdim, kblk, cfn = w_specs[0]
        in_specs.append(w_spec(kdim, kblk, cfn, off=1, pipeline_mode=pl.Buffered(1)))
        w_arrays.append(w)
    n_r = 0
    if resid is not None:
        r_arrays, r_specs = _resid_operands(resid, alpha, tm, tn, lambda j, i: (i, j))
        n_r = len(r_arrays)
        extra_inputs = tuple(r_arrays) + tuple(extra_inputs)
        extra_specs = tuple(r_specs) + tuple(extra_specs)
    return pl.pallas_call(
        functools.partial(_mmw_kernel, n_x=len(xs), mode=mode, alpha=alpha, w_shift=w_shift, w_t=w_t, n_r=n_r,
                          dil=dil),
        grid=(n // tn, t // tm),
        in_specs=in_specs + list(extra_specs),
        out_specs=out_spec,
        out_shape=out_shape,
        scratch_shapes=([pltpu.VMEM((tn, kdim) if w_t else (kdim, tn), BF16) for _, kdim, _, _ in w_specs]
                        + ([pltpu.VMEM((tm, tn), out_dtype)] if dil > 1 else [])),
        compiler_params=_params(("arbitrary", "arbitrary")),
    )(*xs, *w_arrays, *extra_inputs)


def _mmk_kernel(x_ref, w_ref, *refs, alpha, row_chunk):
    r_refs, o_ref, wb_ref = refs[:-2], refs[-2], refs[-1]
    k = pl.program_id(2)

    chunks = [slice(c * row_chunk, (c + 1) * row_chunk) for c in range(o_ref.shape[0] // row_chunk)]

    @pl.when(k == 0)
    def _():
        for rs in chunks:
            o_ref[rs, :] = _resid_term(r_refs, rs, alpha)

    wb_ref[...] = w_ref[...].astype(BF16)
    for rs in chunks:
        o_ref[rs, :] += jnp.dot(x_ref[rs, :], wb_ref[...], preferred_element_type=F32)


def _mmk_resid(x, w, layer, resid, alpha):
    t, kdim = x.shape
    n = w.shape[2]
    tm, tn, tk = _tile(t, 2048), _tile(n, 1024), _tile(kdim, 1024)
    r_arrays, r_specs = _resid_operands(resid, alpha, tm, tn, lambda i, j, k: (i, j))
    return pl.pallas_call(
        functools.partial(_mmk_kernel, alpha=alpha, row_chunk=_tile(tm, 512)),
        grid=(t // tm, n // tn, kdim // tk),
        in_specs=[pl.BlockSpec((tm, tk), lambda i, j, k: (i, k)),
                  pl.BlockSpec((None, tk, tn), lambda i, j, k: (layer, k, j)),
                  *r_specs],
        out_specs=pl.BlockSpec((tm, tn), lambda i, j, k: (i, j)),
        out_shape=jax.ShapeDtypeStruct((t, n), F32),
        scratch_shapes=[pltpu.VMEM((tk, tn), BF16)],
        compiler_params=_params(("parallel", "parallel", "arbitrary")),
    )(x, w, *r_arrays)


def _stream_perm(tm, dil, inverse):
    n = tm // dil
    a = lax.broadcasted_iota(jnp.int32, (tm, tm), 0)
    b = lax.broadcasted_iota(jnp.int32, (tm, tm), 1)
    if inverse:
        src = jnp.bitwise_and(a, dil - 1) * n + lax.shift_right_logical(a, _log2(dil))
    else:
        src = jnp.bitwise_and(a, n - 1) * dil + lax.shift_right_logical(a, _log2(n))
    return b == src


def _ln_kernel(*refs, normalize, emit_f32, emit_bf16):
    y_ref, g_ref, b_ref = refs[:3]
    outs = list(refs[3:])
    f32_ref = outs.pop(0) if emit_f32 else None
    bf16_ref = outs.pop(0) if emit_bf16 else None
    emit_stats = normalize and not emit_f32
    if emit_stats:
        mu_ref, rstd_ref = outs.pop(0), outs.pop(0)
    tm = y_ref.shape[0]
    rows = _tile(tm, LN_ROW_CHUNK)

    def norm_rows(ri, carry):
        rs = pl.ds(pl.multiple_of(ri * rows, rows), rows)
        y = y_ref[rs, :]
        if normalize:
            mu = jnp.mean(y, axis=-1, keepdims=True)
            yc = y - mu
            var = jnp.mean(jnp.square(yc), axis=-1, keepdims=True)
            rstd = lax.rsqrt(var + LN_EPS)
            y = yc * rstd * g_ref[...] + b_ref[...]
            if emit_stats:
                mu_ref[rs, :] = jnp.broadcast_to(mu, (rows, LANES))
                rstd_ref[rs, :] = jnp.broadcast_to(rstd, (rows, LANES))
        if emit_f32:
            f32_ref[rs, :] = y
        if emit_bf16:
            bf16_ref[rs, :] = y.astype(BF16)
        return carry

    lax.fori_loop(0, tm // rows, norm_rows, 0, unroll=min(LN_UNROLL, tm // rows))


def _layer_norm(y3, gain, bias, normalize=True, emit_f32=True, emit_bf16=True):
    b, s, d = y3.shape
    tm = _tile(s, LN_TM)
    tok = pl.BlockSpec((None, tm, d), lambda bi, i: (bi, i, 0))
    vec = pl.BlockSpec((1, d), lambda bi, i: (0, 0))
    out_specs, out_shape = [], []
    if emit_f32:
        out_specs.append(tok)
        out_shape.append(jax.ShapeDtypeStruct((b, s, d), F32))
    if emit_bf16:
        out_specs.append(tok)
        out_shape.append(jax.ShapeDtypeStruct((b, s, d), BF16))
    if normalize and not emit_f32:
        for _ in range(2):
            out_specs.append(pl.BlockSpec((None, tm, LANES), lambda bi, i: (bi, i, 0)))
            out_shape.append(jax.ShapeDtypeStruct((b, s, LANES), F32))
    return pl.pallas_call(
        functools.partial(_ln_kernel, normalize=normalize, emit_f32=emit_f32, emit_bf16=emit_bf16),
        grid=(b, s // tm),
        in_specs=[tok, vec, vec],
        out_specs=out_specs,
        out_shape=out_shape,
        compiler_params=_params(("parallel", "parallel")),
    )(y3, gain.reshape(1, d), bias.reshape(1, d))


def _attn_kernel(q_ref, kc_ref, kp_ref, vc_ref, vp_ref, o_ref, l_ref, s_scr, p_scr, m_scr, bias_scr, *, tq, scale):
    n = pl.program_id(2)
    blk = ATTN_BLOCK
    units = [(h, i) for h in range(ATTN_HPG) for i in range(tq // blk)]
    nt = (((1,), (1,)), ((), ()))

    def keys_vals(ref_c, ref_p, h, i):
        hs = slice(h * HEAD_DIM, (h + 1) * HEAD_DIM)
        if i == 0:
            return jnp.concatenate([ref_p[:, hs], ref_c[:blk, hs]], axis=0)
        return ref_c[(i - 1) * blk:(i + 1) * blk, hs]

    row = lax.broadcasted_iota(jnp.int32, (blk, 2 * blk), 0)
    col = lax.broadcasted_iota(jnp.int32, (blk, 2 * blk), 1)
    bias_scr[...] = jnp.where(col < blk,
                              jnp.where(col >= row, 0.0, -jnp.inf),
                              jnp.where(col - blk <= row, 0.0, -jnp.inf))
    col_r = lax.broadcasted_iota(jnp.int32, (ATTN_ROWS, 2 * blk), 1)
    first_pen = jnp.where(col_r < blk, jnp.where(n == 0, -jnp.inf, 0.0), 0.0)

    for u, (h, i) in enumerate(units):
        q = q_ref[i * blk:(i + 1) * blk, h * HEAD_DIM:(h + 1) * HEAD_DIM]
        s_scr[u] = lax.dot_general(q, keys_vals(kc_ref, kp_ref, h, i), nt, preferred_element_type=F32)

    for u, (h, i) in enumerate(units):
        for r in range(blk // ATTN_ROWS):
            rs = slice(r * ATTN_ROWS, (r + 1) * ATTN_ROWS)
            s = s_scr[u, rs, :] * scale + bias_scr[rs, :]
            if i == 0:
                s = s + first_pen
            m = jnp.max(s, axis=-1, keepdims=True)
            p_scr[u, rs, :] = jnp.exp(s - m).astype(BF16)
            m_scr[u, rs, :] = jnp.broadcast_to(m, (ATTN_ROWS, HEAD_DIM))

    ones = jnp.ones((2 * blk, HEAD_DIM), BF16)
    for u, (h, i) in enumerate(units):
        v_aug = jnp.concatenate([keys_vals(vc_ref, vp_ref, h, i), ones], axis=1)
        pv = jnp.dot(p_scr[u], v_aug, preferred_element_type=F32)
        den = pv[:, HEAD_DIM:]
        rs, hs = slice(i * blk, (i + 1) * blk), slice(h * HEAD_DIM, (h + 1) * HEAD_DIM)
        o_ref[rs, hs] = pv[:, :HEAD_DIM] / den
        l_ref[rs, hs] = m_scr[u] + jnp.log(den)


def _attn_group(qkv):
    b, dil, sub, _ = qkv.shape
    tq = _tile(sub, 512)
    per = tq // ATTN_BLOCK
    n_units = ATTN_HPG * per

    def cur(c):
        return pl.BlockSpec((None, None, tq, D_AGRP), lambda bi, r, n: (bi, r, n, c))

    def prev(c):
        return pl.BlockSpec((None, None, ATTN_BLOCK, D_AGRP),
                            lambda bi, r, n: (bi, r, jnp.maximum(n * per - 1, 0), c))

    out_sds = jax.ShapeDtypeStruct((b, dil, sub, D_AGRP), F32)
    return pl.pallas_call(
        functools.partial(_attn_kernel, tq=tq, scale=HEAD_DIM ** -0.5),
        grid=(b, dil, sub // tq),
        in_specs=[cur(0), cur(1), prev(1), cur(2), prev(2)],
        out_specs=[cur(0), cur(0)],
        out_shape=[out_sds, out_sds],
        scratch_shapes=[pltpu.VMEM((n_units, ATTN_BLOCK, 2 * ATTN_BLOCK), F32),
                        pltpu.VMEM((n_units, ATTN_BLOCK, 2 * ATTN_BLOCK), BF16),
                        pltpu.VMEM((n_units, ATTN_BLOCK, HEAD_DIM), F32),
                        pltpu.VMEM((ATTN_BLOCK, 2 * ATTN_BLOCK), F32)],
        compiler_params=_params(("parallel", "parallel", "arbitrary")),
    )(qkv, qkv, qkv, qkv, qkv)


def _to_token_order(x_ref, dil):
    if dil == 1:
        return x_ref[0]
    n, w = x_ref.shape[1], x_ref.shape[2]
    x = x_ref[...].reshape(dil * n, w)
    perm = jnp.where(_stream_perm(dil * n, dil, inverse=True), 1.0, 0.0).astype(BF16)
    hi = x.astype(BF16)
    r1 = x - hi.astype(F32)
    mid = r1.astype(BF16)
    lo = (r1 - mid.astype(F32)).astype(BF16)
    return (jnp.dot(perm, hi, preferred_element_type=F32)
            + jnp.dot(perm, mid, preferred_element_type=F32)
            + jnp.dot(perm, lo, preferred_element_type=F32))


def _attn_mix_kernel(o0, o1, o2, l0, l1, l2, y_ref):
    os_ = [_to_token_order(o, d) for o, d in zip((o0, o1, o2), ATTN_DILATIONS)]
    ls = [_to_token_order(l, d) for l, d in zip((l0, l1, l2), ATTN_DILATIONS)]
    m = jnp.maximum(jnp.maximum(ls[0], ls[1]), ls[2])
    es = [jnp.exp(l - m) for l in ls]
    inv = 1.0 / (es[0] + es[1] + es[2])
    for g in range(len(ATTN_DILATIONS)):
        y_ref[:, g * D_AGRP:(g + 1) * D_AGRP] = (os_[g] * (es[g] * inv)).astype(y_ref.dtype)


def _attn_mix(os_, ls_):
    b, _, s, _ = os_[0].shape
    tm = _tile(s, PERM_ROWS)
    specs = [pl.BlockSpec((None, d, tm // d, D_AGRP), lambda bi, i: (bi, 0, i, 0)) for d in ATTN_DILATIONS]
    return pl.pallas_call(
        _attn_mix_kernel,
        grid=(b, s // tm),
        in_specs=specs + specs,
        out_specs=pl.BlockSpec((None, tm, D_ATTN), lambda bi, i: (bi, i, 0)),
        out_shape=jax.ShapeDtypeStruct((b, s, D_ATTN), BF16),
        compiler_params=_params(("parallel", "parallel")),
    )(*os_, *ls_)


def _shift_rows(cur, halo, sh):
    rolled = pltpu.roll(cur, sh, 0)
    hr = pltpu.roll(halo, sh, 0)
    row = lax.broadcasted_iota(jnp.int32, halo.shape, 0)
    first = jnp.where(row < sh, hr, rolled[:SUBLANES])
    return jnp.concatenate([first, rolled[SUBLANES:]], axis=0)


def _causal_conv(cur, halo, w):
    kk = w.shape[0]
    acc = w[kk - 1:kk] * cur
    for sh in range(1, kk):
        acc = acc + w[kk - 1 - sh:kk - sh] * _shift_rows(cur, halo, sh)
    return acc


def _sc_kernel(b_ref, c_ref, h_ref, ch_ref, hh_ref, w_ref, o_ref):
    s = pl.program_id(1)
    g = c_ref[...] * h_ref[...]
    gh = ch_ref[...] * hh_ref[...] * jnp.where(s > 0, 1.0, 0.0)
    o_ref[...] = (b_ref[...] * _causal_conv(g, gh, w_ref[...])).astype(o_ref.dtype)


def _short_conv(sc, conv_w):
    b, s, _ = sc.shape
    ts = _tile(s, 512)
    per = ts // SUBLANES

    def cur(j):
        return pl.BlockSpec((None, ts, D_SC), lambda bi, si: (bi, si, j))

    def halo(j):
        return pl.BlockSpec((None, SUBLANES, D_SC), lambda bi, si: (bi, jnp.maximum(si * per - 1, 0), j))

    return pl.pallas_call(
        _sc_kernel,
        grid=(b, s // ts),
        in_specs=[cur(0), cur(1), cur(2), halo(1), halo(2),
                  pl.BlockSpec((SC_CONV, D_SC), lambda bi, si: (0, 0))],
        out_specs=pl.BlockSpec((None, ts, D_SC), lambda bi, si: (bi, si, 0)),
        out_shape=jax.ShapeDtypeStruct((b, s, D_SC), BF16),
        compiler_params=_params(("parallel", "arbitrary")),
    )(sc, sc, sc, sc, sc, conv_w)


def _ssd_kernel(xs_ref, z_ref, b_ref, c_ref, dt_ref,
                wx_ref, bx_ref, wb_ref, bb_ref, wc_ref, bc_ref,
                dtb_ref, alog_ref, dsk_ref, nw_ref,
                o_ref,
                h_ref, xp_ref, bp_ref, cp_ref, y_ref, *, tc):
    g = pl.program_id(1)
    s = pl.program_id(2)
    q = SSM_CHUNK
    p = SSM_HEAD_DIM
    pads = ((xs_ref, xp_ref, wx_ref, bx_ref), (b_ref, bp_ref, wb_ref, bb_ref), (c_ref, cp_ref, wc_ref, bc_ref))

    @pl.when(s == 0)
    def _():
        h_ref[...] = jnp.zeros_like(h_ref)
        for _, pad_ref, _, _ in pads:
            pad_ref[:SUBLANES, :] = jnp.zeros((SUBLANES, pad_ref.shape[1]), F32)

    for raw_ref, pad_ref, _, _ in pads:
        pad_ref[SUBLANES:, :] = raw_ref[...]

    dt_shift = (LANES - g * SSM_HPG) % LANES
    a_row = -jnp.exp(alog_ref[...])
    dskip = dsk_ref[...]
    li = lax.broadcasted_iota(jnp.int32, (q, q), 0)
    si = lax.broadcasted_iota(jnp.int32, (q, q), 1)
    causal = li >= si
    tri = jnp.where(causal, 1.0, 0.0).astype(F32)
    first_head = lax.broadcasted_iota(jnp.int32, (q, 2 * p), 1) < p
    nt = (((1,), (1,)), ((), ()))

    def conv_silu(pad_ref, w_ref, bias_ref, r0):
        w = w_ref[...]
        win = pad_ref[pl.ds(r0, q + SUBLANES), :]
        acc = bias_ref[...]
        for k in range(SSM_CONV):
            back = SSM_CONV - 1 - k
            shifted = win if back == 0 else pltpu.roll(win, back, 0)
            acc = acc + w[k:k + 1] * shifted[SUBLANES:]
        return acc * jax.nn.sigmoid(acc)

    def chunk(ci, carry):
        r0 = pl.multiple_of(ci * q, q)
        xq = conv_silu(xp_ref, wx_ref, bx_ref, r0)
        bq = conv_silu(bp_ref, wb_ref, bb_ref, r0)
        cq = conv_silu(cp_ref, wc_ref, bc_ref, r0)
        dt_raw = pltpu.roll(dt_ref[pl.ds(r0, q), :], dt_shift, 1)
        dtq = jax.nn.softplus(dt_raw + dtb_ref[...])
        cs = jnp.dot(tri, dtq * a_row, precision=lax.Precision.HIGHEST,
                     preferred_element_type=F32)
        cs_t = cs.T
        dt_t = dtq.T
        bb = bq.astype(BF16)
        cb16 = cq.astype(BF16)
        cb = lax.dot_general(cb16, bb, nt, preferred_element_type=F32)
        b_t = bq.T
        hprev = h_ref[...]
        y_off = jnp.dot(cb16, hprev.astype(BF16), preferred_element_type=F32)
        for jp in range(SSM_HPG // 2):
            cols = slice(jp * 2 * p, (jp + 1) * 2 * p)
            x2 = xq[:, cols]
            y_d, st, e_col, e_last = None, None, [], []
            for half in range(2):
                j = 2 * jp + half
                xh = (jnp.where(first_head, x2, 0.0) if half == 0 else jnp.where(first_head, 0.0, x2)).astype(BF16)
                col = cs[:, j:j + 1]
                row = cs_t[j:j + 1, :]
                dtrow = dt_t[j:j + 1, :]
                decay = jnp.exp(jnp.where(causal, col - row, -jnp.inf))
                m = (cb * decay * dtrow).astype(BF16)
                part = jnp.dot(m, xh, preferred_element_type=F32)
                y_d = part if y_d is None else y_d + part
                last = cs[q - 1:q, j:j + 1]
                w_row = jnp.exp(last - row) * dtrow
                part = jnp.dot((b_t * w_row).astype(BF16), xh, preferred_element_type=F32)
                st = part if st is None else st + part
                e_col.append(jnp.exp(col))
                e_last.append(jnp.exp(last))
            y_ref[:, cols] = (y_d + y_off[:, cols] * jnp.where(first_head, e_col[0], e_col[1])
                              + dskip[:, cols] * x2)
            h_ref[:, cols] = hprev[:, cols] * jnp.where(first_head[:1], e_last[0], e_last[1]) + st

        z = z_ref[pl.ds(r0, q), :]
        yv = y_ref[...] * (z * jax.nn.sigmoid(z))
        ms = jnp.mean(jnp.square(yv), axis=-1, keepdims=True)
        o_ref[pl.ds(r0, q), :] = (yv * lax.rsqrt(ms + RMS_EPS) * nw_ref[...]).astype(o_ref.dtype)
        return carry

    lax.fori_loop(0, tc // q, chunk, 0, unroll=min(SSM_UNROLL, tc // q))

    for _, pad_ref, _, _ in pads:
        pad_ref[:SUBLANES, :] = pad_ref[tc:, :]


def _ssd(zx, dt, conv_w, conv_b, dt_bias, a_log, d_skip, norm_w):
    b, s, _ = zx.shape
    tc = _tile(s, 2048)
    gw, ns, hpg = SSM_GW, SSM_STATE, SSM_HPG

    def tok(width, base):
        return pl.BlockSpec((None, tc, width), lambda bi, g, si: (bi, si, base // width + g))

    def par(rows, width):
        return pl.BlockSpec((rows, width), lambda bi, g, si: (0, g))

    def grp(arr):
        a = arr.reshape(SSM_GROUPS, 1, hpg).astype(F32)
        return jnp.pad(a, ((0, 0), (0, 0), (0, LANES - hpg)))

    grp_spec = pl.BlockSpec((None, 1, LANES), lambda bi, g, si: (g, 0, 0))
    wx, wb, wc = conv_w[:, :D_SSM], conv_w[:, D_SSM:D_SSM + 4 * ns], conv_w[:, D_SSM + 4 * ns:]
    cb2 = conv_b.reshape(1, -1)
    bx, bb, bc = cb2[:, :D_SSM], cb2[:, D_SSM:D_SSM + 4 * ns], cb2[:, D_SSM + 4 * ns:]
    d_exp = jnp.repeat(d_skip.astype(F32), SSM_HEAD_DIM).reshape(1, D_SSM)
    return pl.pallas_call(
        functools.partial(_ssd_kernel, tc=tc),
        grid=(b, SSM_GROUPS, s // tc),
        in_specs=[
            tok(gw, ZX_XS), tok(gw, 0), tok(ns, ZX_B), tok(ns, ZX_C),
            pl.BlockSpec((None, tc, LANES), lambda bi, g, si: (bi, si, 0)),
            par(SSM_CONV, gw), par(1, gw),
            par(SSM_CONV, ns), par(1, ns),
            par(SSM_CONV, ns), par(1, ns),
            grp_spec, grp_spec,
            par(1, gw), par(1, gw),
        ],
        out_specs=pl.BlockSpec((None, tc, gw), lambda bi, g, si: (bi, si, g)),
        out_shape=jax.ShapeDtypeStruct((b, s, D_SSM), BF16),
        scratch_shapes=[
            pltpu.VMEM((ns, gw), F32),
            pltpu.VMEM((SUBLANES + tc, gw), F32),
            pltpu.VMEM((SUBLANES + tc, ns), F32),
            pltpu.VMEM((SUBLANES + tc, ns), F32),
            pltpu.VMEM((SSM_CHUNK, gw), F32),
        ],
        compiler_params=_params(("parallel", "parallel", "arbitrary")),
    )(zx, zx, zx, zx, dt,
      wx, bx, wb, bb, wc, bc,
      grp(dt_bias), grp(a_log), d_exp, norm_w.reshape(1, D_SSM).astype(F32))


def _rope_tables(s_len):
    pos = jnp.arange(s_len, dtype=jnp.int32)
    inv_freq = ROPE_THETA ** (-jnp.arange(ROPE_HALF, dtype=F32) / ROPE_HALF)
    ang = pos.astype(F32)[:, None] * inv_freq[None, :]
    cos, sin = jnp.cos(ang), jnp.sin(ang)
    rest = jnp.zeros((s_len, HEAD_DIM - 2 * ROPE_HALF), F32)
    z16 = jnp.zeros((s_len, ROPE_HALF), F32)
    c = jnp.concatenate([cos, cos, jnp.ones_like(rest)], axis=1)
    s1 = jnp.concatenate([-sin, z16, rest], axis=1)
    s2 = jnp.concatenate([z16, sin, rest], axis=1)
    ident = (jnp.ones_like(c), jnp.zeros_like(c), jnp.zeros_like(c))
    return tuple(jnp.stack([t, i]) for t, i in zip((c, s1, s2), ident))


def kernel(x, p, w_in, ssm_conv_w, ssm_conv_b, ssm_dt_bias, ssm_a_log, ssm_d, ssm_norm_w, sc_conv_w,
           w_out, ln1_g, ln1_b, w_up, w_down, ln2_g, ln2_b, w_pe, w_gate, ln3_g, ln3_b):
    bsz, s_len, d_model = x.shape
    depth = w_in.shape[0]
    t = bsz * s_len
    d_ple = p.shape[-1]
    alpha = (2.0 * depth) ** 0.25
    tm = _tile(t, MM_TM)
    assert s_len % tm == 0
    rope_blocks = s_len // tm
    rope = _rope_tables(s_len)
    rope_specs = [pl.BlockSpec((None, tm, HEAD_DIM), lambda j, i: (j // 2, i % rope_blocks, 0))] * 3
    n_grp = len(ATTN_DILATIONS)
    w_in_t = jnp.swapaxes(w_in, 1, 2)

    def flat(a):
        return a.reshape(t, a.shape[-1])

    def ln_resid(y, stats, gain, bias):
        return (flat(y), flat(stats[0]), flat(stats[1]), gain, bias)

    resid = (flat(x),)
    xb, = _layer_norm(x, ln1_g[0], ln1_b[0], normalize=False, emit_f32=False)
    for i in range(depth):
        xbf = flat(xb)
        outs = []
        for g, dil in enumerate(ATTN_DILATIONS):
            qkv = _mmw([xbf], [(w_in_t, d_model, 0, lambda j, g=g: j * n_grp + g)], i,
                       3 * D_AGRP, BF16, mode="rope", extra_inputs=rope, extra_specs=rope_specs,
                       w_t=True, stream=(bsz, dil))
            outs.append(_attn_group(qkv.reshape(bsz, dil, s_len // dil, 3 * D_AGRP)))
        y_attn = _attn_mix([o for o, _ in outs], [l for _, l in outs])

        zx = _mmw([xbf], [(w_in_t, d_model, 0, lambda j: U_Z // MM_TN + j)], i, D_ZX, F32, w_t=True)
        dt = _mmw([xbf], [(w_in_t, d_model, 0, lambda j: U_DT // LANES)], i, LANES, F32, tn=LANES, w_t=True)
        sc = _mmw([xbf], [(w_in_t, d_model, 0, lambda j: U_DT // MM_TN + j)], i, 3 * D_SC, F32,
                  w_shift=U_SC - U_DT, w_t=True)
        y_ssm = _ssd(zx.reshape(bsz, s_len, D_ZX), dt.reshape(bsz, s_len, LANES), ssm_conv_w[i], ssm_conv_b[i],
                     ssm_dt_bias[i], ssm_a_log[i], ssm_d[i], ssm_norm_w[i])
        y_sc = _short_conv(sc.reshape(bsz, s_len, 3 * D_SC), sc_conv_w[i])

        y1 = _mmw([flat(y_attn), flat(y_ssm), flat(y_sc)],
                  [(w_out, D_ATTN, 0, lambda j: j), (w_out, D_SSM, 1, lambda j: j),
                   (w_out, D_SC, (D_ATTN + D_SSM) // D_SC, lambda j: j)], i,
                  d_model, F32, mode="resid", resid=resid, alpha=alpha).reshape(bsz, s_len, d_model)
        xb, *stats = _layer_norm(y1, ln1_g[i], ln1_b[i], emit_f32=False)
        resid = ln_resid(y1, stats, ln1_g[i], ln1_b[i])

        hid = _mmw([flat(xb)], [(w_up, d_model, 0, lambda j: j)], i, w_up.shape[2], BF16, mode="relu2")
        y2 = _mmk_resid(hid, w_down, i, resid, alpha).reshape(bsz, s_len, d_model)
        xb, *stats = _layer_norm(y2, ln2_g[i], ln2_b[i], emit_f32=False)
        resid = ln_resid(y2, stats, ln2_g[i], ln2_b[i])

        y3 = _mmw([flat(xb)], [(w_gate, d_model, 0, lambda j: j)], i, d_model, F32, mode="gate",
                  resid=resid, extra_inputs=(flat(p[i]), w_pe[i]),
                  extra_specs=(pl.BlockSpec((tm, d_ple), lambda j, i_: (i_, 0)),
                               pl.BlockSpec((d_ple, MM_TN), lambda j, i_: (0, j))),
                  alpha=alpha).reshape(bsz, s_len, d_model)
        if i == depth - 1:
            out, = _layer_norm(y3, ln3_g[i], ln3_b[i], emit_bf16=False)
            return out
        xb, *stats = _layer_norm(y3, ln3_g[i], ln3_b[i], emit_f32=False)
        resid = ln_resid(y3, stats, ln3_g[i], ln3_b[i])
```

```python
import functools

import jax
import jax.numpy as jnp
from jax import lax
from jax.experimental import pallas as pl
from jax.experimental.pallas import tpu as pltpu

HEAD_DIM = 128
ATTN_DILATIONS = (1, 4, 16)
ATTN_HPG = 4
ATTN_BLOCK = 128
ATTN_ROWS = 32
D_ATTN = ATTN_HPG * len(ATTN_DILATIONS) * HEAD_DIM
D_AGRP = ATTN_HPG * HEAD_DIM
ROPE_THETA = 500000.0
ROPE_HALF = HEAD_DIM // 8
D_SSM = 1536
SSM_HEAD_DIM = 64
SSM_HEADS = D_SSM // SSM_HEAD_DIM
SSM_GROUPS = 4
SSM_HPG = SSM_HEADS // SSM_GROUPS
SSM_GW = D_SSM // SSM_GROUPS
SSM_STATE = 128
SSM_CONV = 4
SSM_CHUNK = 128
SSM_UNROLL = 2
D_SC = 1024
SC_CONV = 3
LN_EPS = 1e-5
RMS_EPS = 1e-5

U_Z = 3 * D_ATTN
U_DT = U_Z + 2 * D_SSM + 2 * SSM_GROUPS * SSM_STATE
U_SC = U_DT + SSM_HEADS
D_ZX = U_DT - U_Z
ZX_XS = D_SSM
ZX_B = 2 * D_SSM
ZX_C = ZX_B + SSM_GROUPS * SSM_STATE

LANES = 128
SUBLANES = 8
VMEM_LIMIT_BYTES = 56 * 1024 * 1024

BF16 = jnp.bfloat16
F32 = jnp.float32

MM_TM = 1024
MM_TN = 512
LN_TM = 512
LN_ROW_CHUNK = 8
LN_UNROLL = 16
PERM_ROWS = 256


def _params(sem):
    return pltpu.CompilerParams(dimension_semantics=sem, vmem_limit_bytes=VMEM_LIMIT_BYTES)


def _tile(n, pref):
    t = min(n, pref)
    while n % t:
        t -= 1
    return t


def _log2(n):
    assert n > 0 and n & (n - 1) == 0, n
    return n.bit_length() - 1


def _resid_term(r_refs, rs, alpha):
    if len(r_refs) == 1:
        return alpha * r_refs[0][rs, :]
    y_ref, mu_ref, rstd_ref, ag_ref, ab_ref = r_refs
    rep = y_ref.shape[1] // LANES
    mu = jnp.tile(mu_ref[rs, :], (1, rep))
    rstd = jnp.tile(rstd_ref[rs, :], (1, rep))
    return (y_ref[rs, :] - mu) * rstd * ag_ref[...] + ab_ref[...]


def _resid_operands(resid, alpha, tm, tn, index):
    if len(resid) == 1:
        return [resid[0]], [pl.BlockSpec((tm, tn), index)]
    y, mu, rstd, gain, bias = resid
    d = y.shape[1]

    def rows(*g):
        return (index(*g)[0], 0)

    def cols(*g):
        return (0, index(*g)[1])

    return ([y, mu, rstd, (alpha * gain).reshape(1, d), (alpha * bias).reshape(1, d)],
            [pl.BlockSpec((tm, tn), index), pl.BlockSpec((tm, LANES), rows), pl.BlockSpec((tm, LANES), rows),
             pl.BlockSpec((1, tn), cols), pl.BlockSpec((1, tn), cols)])


def _mmw_kernel(*refs, n_x, mode, alpha, w_shift, w_t, n_r, dil):
    xs, ws = refs[:n_x], refs[n_x:2 * n_x]
    pos = 2 * n_x
    if w_shift:
        w_next_ref = refs[pos]
        pos += 1
    if mode == "rope":
        c_ref, s1_ref, s2_ref = refs[pos:pos + 3]
        pos += 3
    elif mode == "resid":
        r_refs = refs[pos:pos + n_r]
        pos += n_r
    elif mode == "gate":
        r_refs = refs[pos:pos + n_r]
        p_ref, wpe_ref = refs[pos + n_r:pos + n_r + 2]
        pos += n_r + 2
    o_ref = refs[pos]
    wbs = refs[pos + 1:pos + 1 + n_x]
    stage_ref = refs[pos + 1 + n_x] if dil > 1 else None

    @pl.when(pl.program_id(1) == 0)
    def _():
        if w_shift:
            keep = wbs[0].shape[0] - w_shift
            wbs[0][:keep, :] = ws[0][w_shift:, :].astype(BF16)
            wbs[0][keep:, :] = w_next_ref[:w_shift, :].astype(BF16)
        else:
            for w_ref, wb_ref in zip(ws, wbs):
                wb_ref[...] = w_ref[...].astype(BF16)

    contract = (((1,), (1,)), ((), ())) if w_t else (((1,), (0,)), ((), ()))
    tm, tn = xs[0].shape[0], o_ref.shape[-1]
    halves = {"rope": 4, "gate": 4, "resid": 4}.get(mode, 1)
    halves = halves if tm % (halves * 2 * SUBLANES) == 0 else 1
    rows_per = tm // halves
    if dil > 1:
        rows_per = min(tm, PERM_ROWS)
        halves = tm // rows_per
        perm = jnp.where(_stream_perm(rows_per, dil, inverse=False), 1.0, 0.0).astype(BF16)
    if mode in ("resid", "gate"):
        for hf in range(halves):
            rs = slice(hf * rows_per, (hf + 1) * rows_per)
            o_ref[rs, :] = _resid_term(r_refs, rs, alpha)

    for hf in range(halves):
        rs = slice(hf * rows_per, (hf + 1) * rows_per)
        acc = None
        for x_ref, wb_ref in zip(xs, wbs):
            part = lax.dot_general(x_ref[rs, :], wb_ref[...], contract, preferred_element_type=F32)
            acc = part if acc is None else acc + part

        if mode == "rope":
            rep = tn // HEAD_DIM
            acc = (acc * jnp.tile(c_ref[rs, :], (1, rep))
                   + pltpu.roll(acc, tn - ROPE_HALF, 1) * jnp.tile(s1_ref[rs, :], (1, rep))
                   + pltpu.roll(acc, ROPE_HALF, 1) * jnp.tile(s2_ref[rs, :], (1, rep)))
        elif mode == "relu2":
            acc = jnp.square(jnp.maximum(acc, 0.0))
        elif mode == "resid":
            acc = o_ref[rs, :] + acc
        elif mode == "gate":
            emb = jnp.dot(p_ref[rs, :].astype(BF16), wpe_ref[...].astype(BF16), preferred_element_type=F32)
            acc = o_ref[rs, :] + jax.nn.sigmoid(acc) * emb
        (stage_ref if dil > 1 else o_ref)[rs, :] = acc.astype(o_ref.dtype)

    if dil > 1:
        n = rows_per // dil
        for hf in range(halves):
            rs = slice(hf * rows_per, (hf + 1) * rows_per)
            moved = jnp.dot(perm, stage_ref[rs, :], preferred_element_type=F32).astype(o_ref.dtype)
            for r in range(dil):
                o_ref[r, hf * n:(hf + 1) * n, :] = moved[r * n:(r + 1) * n]


def _mmw(xs, w_specs, layer, n, out_dtype, mode="plain", extra_inputs=(), extra_specs=(), alpha=1.0, tn=MM_TN,
         w_shift=0, w_t=False, resid=None, stream=None):
    t = xs[0].shape[0]
    tm = _tile(t, MM_TM)
    assert n % tn == 0
    dil = 1
    out_spec = pl.BlockSpec((tm, tn), lambda j, i: (i, j))
    out_shape = jax.ShapeDtypeStruct((t, n), out_dtype)
    if stream is not None and stream[1] > 1:
        batch, dil = stream
        per_seq = t // batch // tm
        assert tm % PERM_ROWS == 0 or tm < PERM_ROWS
        out_spec = pl.BlockSpec((None, dil, tm // dil, tn), lambda j, i: (i // per_seq, 0, i % per_seq, j))
        out_shape = jax.ShapeDtypeStruct((batch, dil, t // batch // dil, n), out_dtype)
    in_specs = [pl.BlockSpec((tm, x.shape[1]), lambda j, i: (i, 0)) for x in xs]
    w_arrays = [w for w, _, _, _ in w_specs]

    def w_spec(kdim, kblk, cfn, off=0, **kw):
        if w_t:
            return pl.BlockSpec((None, tn, kdim), lambda j, i: (layer, cfn(j) + off, kblk), **kw)
        return pl.BlockSpec((None, kdim, tn), lambda j, i: (layer, kblk, cfn(j) + off), **kw)

    for _, kdim, kblk, cfn in w_specs:
        in_specs.append(w_spec(kdim, kblk, cfn))
    if w_shift:
        assert w_t and len(w_specs) == 1 and w_shift % SUBLANES == 0
        w, kdim, kblk, cfn = w_specs[0]
        in_specs.append(w_spec(kdim, kblk, cfn, off=1, pipeline_mode=pl.Buffered(1)))
        w_arrays.append(w)
    n_r = 0
    if resid is not None:
        r_arrays, r_specs = _resid_operands(resid, alpha, tm, tn, lambda j, i: (i, j))
        n_r = len(r_arrays)
        extra_inputs = tuple(r_arrays) + tuple(extra_inputs)
        extra_specs = tuple(r_specs) + tuple(extra_specs)
    return pl.pallas_call(
        functools.partial(_mmw_kernel, n_x=len(xs), mode=mode, alpha=alpha, w_shift=w_shift, w_t=w_t, n_r=n_r,
                          dil=dil),
        grid=(n // tn, t // tm),
        in_specs=in_specs + list(extra_specs),
        out_specs=out_spec,
        out_shape=out_shape,
        scratch_shapes=([pltpu.VMEM((tn, kdim) if w_t else (kdim, tn), BF16) for _, kdim, _, _ in w_specs]
                        + ([pltpu.VMEM((tm, tn), out_dtype)] if dil > 1 else [])),
        compiler_params=_params(("arbitrary", "arbitrary")),
    )(*xs, *w_arrays, *extra_inputs)


def _mmk_kernel(x_ref, w_ref, *refs, alpha, row_chunk):
    r_refs, o_ref, wb_ref = refs[:-2], refs[-2], refs[-1]
    k = pl.program_id(2)

    chunks = [slice(c * row_chunk, (c + 1) * row_chunk) for c in range(o_ref.shape[0] // row_chunk)]

    @pl.when(k == 0)
    def _():
        for rs in chunks:
            o_ref[rs, :] = _resid_term(r_refs, rs, alpha)

    wb_ref[...] = w_ref[...].astype(BF16)
    for rs in chunks:
        o_ref[rs, :] += jnp.dot(x_ref[rs, :], wb_ref[...], preferred_element_type=F32)


def _mmk_resid(x, w, layer, resid, alpha):
    t, kdim = x.shape
    n = w.shape[2]
    tm, tn, tk = _tile(t, 2048), _tile(n, 1024), _tile(kdim, 1024)
    r_arrays, r_specs = _resid_operands(resid, alpha, tm, tn, lambda i, j, k: (i, j))
    return pl.pallas_call(
        functools.partial(_mmk_kernel, alpha=alpha, row_chunk=_tile(tm, 512)),
        grid=(t // tm, n // tn, kdim // tk),
        in_specs=[pl.BlockSpec((tm, tk), lambda i, j, k: (i, k)),
                  pl.BlockSpec((None, tk, tn), lambda i, j, k: (layer, k, j)),
                  *r_specs],
        out_specs=pl.BlockSpec((tm, tn), lambda i, j, k: (i, j)),
        out_shape=jax.ShapeDtypeStruct((t, n), F32),
        scratch_shapes=[pltpu.VMEM((tk, tn), BF16)],
        compiler_params=_params(("parallel", "parallel", "arbitrary")),
    )(x, w, *r_arrays)


def _stream_perm(tm, dil, inverse):
    n = tm // dil
    a = lax.broadcasted_iota(jnp.int32, (tm, tm), 0)
    b = lax.broadcasted_iota(jnp.int32, (tm, tm), 1)
    if inverse:
        src = jnp.bitwise_and(a, dil - 1) * n + lax.shift_right_logical(a, _log2(dil))
    else:
        src = jnp.bitwise_and(a, n - 1) * dil + lax.shift_right_logical(a, _log2(n))
    return b == src


def _ln_kernel(*refs, normalize, emit_f32, emit_bf16):
    y_ref, g_ref, b_ref = refs[:3]
    outs = list(refs[3:])
    f32_ref = outs.pop(0) if emit_f32 else None
    bf16_ref = outs.pop(0) if emit_bf16 else None
    emit_stats = normalize and not emit_f32
    if emit_stats:
        mu_ref, rstd_ref = outs.pop(0), outs.pop(0)
    tm = y_ref.shape[0]
    rows = _tile(tm, LN_ROW_CHUNK)

    def norm_rows(ri, carry):
        rs = pl.ds(pl.multiple_of(ri * rows, rows), rows)
        y = y_ref[rs, :]
        if normalize:
            mu = jnp.mean(y, axis=-1, keepdims=True)
            yc = y - mu
            var = jnp.mean(jnp.square(yc), axis=-1, keepdims=True)
            rstd = lax.rsqrt(var + LN_EPS)
            y = yc * rstd * g_ref[...] + b_ref[...]
            if emit_stats:
                mu_ref[rs, :] = jnp.broadcast_to(mu, (rows, LANES))
                rstd_ref[rs, :] = jnp.broadcast_to(rstd, (rows, LANES))
        if emit_f32:
            f32_ref[rs, :] = y
        if emit_bf16:
            bf16_ref[rs, :] = y.astype(BF16)
        return carry

    lax.fori_loop(0, tm // rows, norm_rows, 0, unroll=min(LN_UNROLL, tm // rows))


def _layer_norm(y3, gain, bias, normalize=True, emit_f32=True, emit_bf16=True):
    b, s, d = y3.shape
    tm = _tile(s, LN_TM)
    tok = pl.BlockSpec((None, tm, d), lambda bi, i: (bi, i, 0))
    vec = pl.BlockSpec((1, d), lambda bi, i: (0, 0))
    out_specs, out_shape = [], []
    if emit_f32:
        out_specs.append(tok)
        out_shape.append(jax.ShapeDtypeStruct((b, s, d), F32))
    if emit_bf16:
        out_specs.append(tok)
        out_shape.append(jax.ShapeDtypeStruct((b, s, d), BF16))
    if normalize and not emit_f32:
        for _ in range(2):
            out_specs.append(pl.BlockSpec((None, tm, LANES), lambda bi, i: (bi, i, 0)))
            out_shape.append(jax.ShapeDtypeStruct((b, s, LANES), F32))
    return pl.pallas_call(
        functools.partial(_ln_kernel, normalize=normalize, emit_f32=emit_f32, emit_bf16=emit_bf16),
        grid=(b, s // tm),
        in_specs=[tok, vec, vec],
        out_specs=out_specs,
        out_shape=out_shape,
        compiler_params=_params(("parallel", "parallel")),
    )(y3, gain.reshape(1, d), bias.reshape(1, d))


def _attn_kernel(q_ref, kc_ref, kp_ref, vc_ref, vp_ref, o_ref, l_ref, s_scr, p_scr, m_scr, bias_scr, *, tq, scale):
    n = pl.program_id(2)
    blk = ATTN_BLOCK
    units = [(h, i) for h in range(ATTN_HPG) for i in range(tq // blk)]
    nt = (((1,), (1,)), ((), ()))

    def keys_vals(ref_c, ref_p, h, i):
        hs = slice(h * HEAD_DIM, (h + 1) * HEAD_DIM)
        if i == 0:
            return jnp.concatenate([ref_p[:, hs], ref_c[:blk, hs]], axis=0)
        return ref_c[(i - 1) * blk:(i + 1) * blk, hs]

    row = lax.broadcasted_iota(jnp.int32, (blk, 2 * blk), 0)
    col = lax.broadcasted_iota(jnp.int32, (blk, 2 * blk), 1)
    bias_scr[...] = jnp.where(col < blk,
                              jnp.where(col >= row, 0.0, -jnp.inf),
                              jnp.where(col - blk <= row, 0.0, -jnp.inf))
    col_r = lax.broadcasted_iota(jnp.int32, (ATTN_ROWS, 2 * blk), 1)
    first_pen = jnp.where(col_r < blk, jnp.where(n == 0, -jnp.inf, 0.0), 0.0)

    for u, (h, i) in enumerate(units):
        q = q_ref[i * blk:(i + 1) * blk, h * HEAD_DIM:(h + 1) * HEAD_DIM]
        s_scr[u] = lax.dot_general(q, keys_vals(kc_ref, kp_ref, h, i), nt, preferred_element_type=F32)

    for u, (h, i) in enumerate(units):
        for r in range(blk // ATTN_ROWS):
            rs = slice(r * ATTN_ROWS, (r + 1) * ATTN_ROWS)
            s = s_scr[u, rs, :] * scale + bias_scr[rs, :]
            if i == 0:
                s = s + first_pen
            m = jnp.max(s, axis=-1, keepdims=True)
            p_scr[u, rs, :] = jnp.exp(s - m).astype(BF16)
            m_scr[u, rs, :] = jnp.broadcast_to(m, (ATTN_ROWS, HEAD_DIM))

    ones = jnp.ones((2 * blk, HEAD_DIM), BF16)
    for u, (h, i) in enumerate(units):
        v_aug = jnp.concatenate([keys_vals(vc_ref, vp_ref, h, i), ones], axis=1)
        pv = jnp.dot(p_scr[u], v_aug, preferred_element_type=F32)
        den = pv[:, HEAD_DIM:]
        rs, hs = slice(i * blk, (i + 1) * blk), slice(h * HEAD_DIM, (h + 1) * HEAD_DIM)
        o_ref[rs, hs] = pv[:, :HEAD_DIM] / den
        l_ref[rs, hs] = m_scr[u] + jnp.log(den)


def _attn_group(qkv):
    b, dil, sub, _ = qkv.shape
    tq = _tile(sub, 512)
    per = tq // ATTN_BLOCK
    n_units = ATTN_HPG * per

    def cur(c):
        return pl.BlockSpec((None, None, tq, D_AGRP), lambda bi, r, n: (bi, r, n, c))

    def prev(c):
        return pl.BlockSpec((None, None, ATTN_BLOCK, D_AGRP),
                            lambda bi, r, n: (bi, r, jnp.maximum(n * per - 1, 0), c))

    out_sds = jax.ShapeDtypeStruct((b, dil, sub, D_AGRP), F32)
    return pl.pallas_call(
        functools.partial(_attn_kernel, tq=tq, scale=HEAD_DIM ** -0.5),
        grid=(b, dil, sub // tq),
        in_specs=[cur(0), cur(1), prev(1), cur(2), prev(2)],
        out_specs=[cur(0), cur(0)],
        out_shape=[out_sds, out_sds],
        scratch_shapes=[pltpu.VMEM((n_units, ATTN_BLOCK, 2 * ATTN_BLOCK), F32),
                        pltpu.VMEM((n_units, ATTN_BLOCK, 2 * ATTN_BLOCK), BF16),
                        pltpu.VMEM((n_units, ATTN_BLOCK, HEAD_DIM), F32),
                        pltpu.VMEM((ATTN_BLOCK, 2 * ATTN_BLOCK), F32)],
        compiler_params=_params(("parallel", "parallel", "arbitrary")),
    )(qkv, qkv, qkv, qkv, qkv)


def _to_token_order(x_ref, dil):
    if dil == 1:
        return x_ref[0]
    n, w = x_ref.shape[1], x_ref.shape[2]
    x = x_ref[...].reshape(dil * n, w)
    perm = jnp.where(_stream_perm(dil * n, dil, inverse=True), 1.0, 0.0).astype(BF16)
    hi = x.astype(BF16)
    r1 = x - hi.astype(F32)
    mid = r1.astype(BF16)
    lo = (r1 - mid.astype(F32)).astype(BF16)
    return (jnp.dot(perm, hi, preferred_element_type=F32)
            + jnp.dot(perm, mid, preferred_element_type=F32)
            + jnp.dot(perm, lo, preferred_element_type=F32))


def _attn_mix_kernel(o0, o1, o2, l0, l1, l2, y_ref):
    os_ = [_to_token_order(o, d) for o, d in zip((o0, o1, o2), ATTN_DILATIONS)]
    ls = [_to_token_order(l, d) for l, d in zip((l0, l1, l2), ATTN_DILATIONS)]
    m = jnp.maximum(jnp.maximum(ls[0], ls[1]), ls[2])
    es = [jnp.exp(l - m) for l in ls]
    inv = 1.0 / (es[0] + es[1] + es[2])
    for g in range(len(ATTN_DILATIONS)):
        y_ref[:, g * D_AGRP:(g + 1) * D_AGRP] = (os_[g] * (es[g] * inv)).astype(y_ref.dtype)


def _attn_mix(os_, ls_):
    b, _, s, _ = os_[0].shape
    tm = _tile(s, PERM_ROWS)
    specs = [pl.BlockSpec((None, d, tm // d, D_AGRP), lambda bi, i: (bi, 0, i, 0)) for d in ATTN_DILATIONS]
    return pl.pallas_call(
        _attn_mix_kernel,
        grid=(b, s // tm),
        in_specs=specs + specs,
        out_specs=pl.BlockSpec((None, tm, D_ATTN), lambda bi, i: (bi, i, 0)),
        out_shape=jax.ShapeDtypeStruct((b, s, D_ATTN), BF16),
        compiler_params=_params(("parallel", "parallel")),
    )(*os_, *ls_)


def _shift_rows(cur, halo, sh):
    rolled = pltpu.roll(cur, sh, 0)
    hr = pltpu.roll(halo, sh, 0)
    row = lax.broadcasted_iota(jnp.int32, halo.shape, 0)
    first = jnp.where(row < sh, hr, rolled[:SUBLANES])
    return jnp.concatenate([first, rolled[SUBLANES:]], axis=0)


def _causal_conv(cur, halo, w):
    kk = w.shape[0]
    acc = w[kk - 1:kk] * cur
    for sh in range(1, kk):
        acc = acc + w[kk - 1 - sh:kk - sh] * _shift_rows(cur, halo, sh)
    return acc


def _sc_kernel(b_ref, c_ref, h_ref, ch_ref, hh_ref, w_ref, o_ref):
    s = pl.program_id(1)
    g = c_ref[...] * h_ref[...]
    gh = ch_ref[...] * hh_ref[...] * jnp.where(s > 0, 1.0, 0.0)
    o_ref[...] = (b_ref[...] * _causal_conv(g, gh, w_ref[...])).astype(o_ref.dtype)


def _short_conv(sc, conv_w):
    b, s, _ = sc.shape
    ts = _tile(s, 512)
    per = ts // SUBLANES

    def cur(j):
        return pl.BlockSpec((None, ts, D_SC), lambda bi, si: (bi, si, j))

    def halo(j):
        return pl.BlockSpec((None, SUBLANES, D_SC), lambda bi, si: (bi, jnp.maximum(si * per - 1, 0), j))

    return pl.pallas_call(
        _sc_kernel,
        grid=(b, s // ts),
        in_specs=[cur(0), cur(1), cur(2), halo(1), halo(2),
                  pl.BlockSpec((SC_CONV, D_SC), lambda bi, si: (0, 0))],
        out_specs=pl.BlockSpec((None, ts, D_SC), lambda bi, si: (bi, si, 0)),
        out_shape=jax.ShapeDtypeStruct((b, s, D_SC), BF16),
        compiler_params=_params(("parallel", "arbitrary")),
    )(sc, sc, sc, sc, sc, conv_w)


def _ssd_kernel(xs_ref, z_ref, b_ref, c_ref, dt_ref,
                wx_ref, bx_ref, wb_ref, bb_ref, wc_ref, bc_ref,
                dtb_ref, alog_ref, dsk_ref, nw_ref,
                o_ref,
                h_ref, xp_ref, bp_ref, cp_ref, y_ref, *, tc):
    g = pl.program_id(1)
    s = pl.program_id(2)
    q = SSM_CHUNK
    p = SSM_HEAD_DIM
    pads = ((xs_ref, xp_ref, wx_ref, bx_ref), (b_ref, bp_ref, wb_ref, bb_ref), (c_ref, cp_ref, wc_ref, bc_ref))

    @pl.when(s == 0)
    def _():
        h_ref[...] = jnp.zeros_like(h_ref)
        for _, pad_ref, _, _ in pads:
            pad_ref[:SUBLANES, :] = jnp.zeros((SUBLANES, pad_ref.shape[1]), F32)

    for raw_ref, pad_ref, _, _ in pads:
        pad_ref[SUBLANES:, :] = raw_ref[...]

    dt_shift = (LANES - g * SSM_HPG) % LANES
    a_row = -jnp.exp(alog_ref[...])
    dskip = dsk_ref[...]
    li = lax.broadcasted_iota(jnp.int32, (q, q), 0)
    si = lax.broadcasted_iota(jnp.int32, (q, q), 1)
    causal = li >= si
    tri = jnp.where(causal, 1.0, 0.0).astype(F32)
    first_head = lax.broadcasted_iota(jnp.int32, (q, 2 * p), 1) < p
    nt = (((1,), (1,)), ((), ()))

    def conv_silu(pad_ref, w_ref, bias_ref, r0):
        w = w_ref[...]
        win = pad_ref[pl.ds(r0, q + SUBLANES), :]
        acc = bias_ref[...]
        for k in range(SSM_CONV):
            back = SSM_CONV - 1 - k
            shifted = win if back == 0 else pltpu.roll(win, back, 0)
            acc = acc + w[k:k + 1] * shifted[SUBLANES:]
        return acc * jax.nn.sigmoid(acc)

    def chunk(ci, carry):
        r0 = pl.multiple_of(ci * q, q)
        xq = conv_silu(xp_ref, wx_ref, bx_ref, r0)
        bq = conv_silu(bp_ref, wb_ref, bb_ref, r0)
        cq = conv_silu(cp_ref, wc_ref, bc_ref, r0)
        dt_raw = pltpu.roll(dt_ref[pl.ds(r0, q), :], dt_shift, 1)
        dtq = jax.nn.softplus(dt_raw + dtb_ref[...])
        cs = jnp.dot(tri, dtq * a_row, precision=lax.Precision.HIGHEST,
                     preferred_element_type=F32)
        cs_t = cs.T
        dt_t = dtq.T
        bb = bq.astype(BF16)
        cb16 = cq.astype(BF16)
        cb = lax.dot_general(cb16, bb, nt, preferred_element_type=F32)
        b_t = bq.T
        hprev = h_ref[...]
        y_off = jnp.dot(cb16, hprev.astype(BF16), preferred_element_type=F32)
        for jp in range(SSM_HPG // 2):
            cols = slice(jp * 2 * p, (jp + 1) * 2 * p)
            x2 = xq[:, cols]
            y_d, st, e_col, e_last = None, None, [], []
            for half in range(2):
                j = 2 * jp + half
                xh = (jnp.where(first_head, x2, 0.0) if half == 0 else jnp.where(first_head, 0.0, x2)).astype(BF16)
                col = cs[:, j:j + 1]
                row = cs_t[j:j + 1, :]
                dtrow = dt_t[j:j + 1, :]
                decay = jnp.exp(jnp.where(causal, col - row, -jnp.inf))
                m = (cb * decay * dtrow).astype(BF16)
                part = jnp.dot(m, xh, preferred_element_type=F32)
                y_d = part if y_d is None else y_d + part
                last = cs[q - 1:q, j:j + 1]
                w_row = jnp.exp(last - row) * dtrow
                part = jnp.dot((b_t * w_row).astype(BF16), xh, preferred_element_type=F32)
                st = part if st is None else st + part
                e_col.append(jnp.exp(col))
                e_last.append(jnp.exp(last))
            y_ref[:, cols] = (y_d + y_off[:, cols] * jnp.where(first_head, e_col[0], e_col[1])
                              + dskip[:, cols] * x2)
            h_ref[:, cols] = hprev[:, cols] * jnp.where(first_head[:1], e_last[0], e_last[1]) + st

        z = z_ref[pl.ds(r0, q), :]
        yv = y_ref[...] * (z * jax.nn.sigmoid(z))
        ms = jnp.mean(jnp.square(yv), axis=-1, keepdims=True)
        o_ref[pl.ds(r0, q), :] = (yv * lax.rsqrt(ms + RMS_EPS) * nw_ref[...]).astype(o_ref.dtype)
        return carry

    lax.fori_loop(0, tc // q, chunk, 0, unroll=min(SSM_UNROLL, tc // q))

    for _, pad_ref, _, _ in pads:
        pad_ref[:SUBLANES, :] = pad_ref[tc:, :]


def _ssd(zx, dt, conv_w, conv_b, dt_bias, a_log, d_skip, norm_w):
    b, s, _ = zx.shape
    tc = _tile(s, 2048)
    gw, ns, hpg = SSM_GW, SSM_STATE, SSM_HPG

    def tok(width, base):
        return pl.BlockSpec((None, tc, width), lambda bi, g, si: (bi, si, base // width + g))

    def par(rows, width):
        return pl.BlockSpec((rows, width), lambda bi, g, si: (0, g))

    def grp(arr):
        a = arr.reshape(SSM_GROUPS, 1, hpg).astype(F32)
        return jnp.pad(a, ((0, 0), (0, 0), (0, LANES - hpg)))

    grp_spec = pl.BlockSpec((None, 1, LANES), lambda bi, g, si: (g, 0, 0))
    wx, wb, wc = conv_w[:, :D_SSM], conv_w[:, D_SSM:D_SSM + 4 * ns], conv_w[:, D_SSM + 4 * ns:]
    cb2 = conv_b.reshape(1, -1)
    bx, bb, bc = cb2[:, :D_SSM], cb2[:, D_SSM:D_SSM + 4 * ns], cb2[:, D_SSM + 4 * ns:]
    d_exp = jnp.repeat(d_skip.astype(F32), SSM_HEAD_DIM).reshape(1, D_SSM)
    return pl.pallas_call(
        functools.partial(_ssd_kernel, tc=tc),
        grid=(b, SSM_GROUPS, s // tc),
        in_specs=[
            tok(gw, ZX_XS), tok(gw, 0), tok(ns, ZX_B), tok(ns, ZX_C),
            pl.BlockSpec((None, tc, LANES), lambda bi, g, si: (bi, si, 0)),
            par(SSM_CONV, gw), par(1, gw),
            par(SSM_CONV, ns), par(1, ns),
            par(SSM_CONV, ns), par(1, ns),
            grp_spec, grp_spec,
            par(1, gw), par(1, gw),
        ],
        out_specs=pl.BlockSpec((None, tc, gw), lambda bi, g, si: (bi, si, g)),
        out_shape=jax.ShapeDtypeStruct((b, s, D_SSM), BF16),
        scratch_shapes=[
            pltpu.VMEM((ns, gw), F32),
            pltpu.VMEM((SUBLANES + tc, gw), F32),
            pltpu.VMEM((SUBLANES + tc, ns), F32),
            pltpu.VMEM((SUBLANES + tc, ns), F32),
            pltpu.VMEM((SSM_CHUNK, gw), F32),
        ],
        compiler_params=_params(("parallel", "parallel", "arbitrary")),
    )(zx, zx, zx, zx, dt,
      wx, bx, wb, bb, wc, bc,
      grp(dt_bias), grp(a_log), d_exp, norm_w.reshape(1, D_SSM).astype(F32))


def _rope_tables(s_len):
    pos = jnp.arange(s_len, dtype=jnp.int32)
    inv_freq = ROPE_THETA ** (-jnp.arange(ROPE_HALF, dtype=F32) / ROPE_HALF)
    ang = pos.astype(F32)[:, None] * inv_freq[None, :]
    cos, sin = jnp.cos(ang), jnp.sin(ang)
    rest = jnp.zeros((s_len, HEAD_DIM - 2 * ROPE_HALF), F32)
    z16 = jnp.zeros((s_len, ROPE_HALF), F32)
    c = jnp.concatenate([cos, cos, jnp.ones_like(rest)], axis=1)
    s1 = jnp.concatenate([-sin, z16, rest], axis=1)
    s2 = jnp.concatenate([z16, sin, rest], axis=1)
    ident = (jnp.ones_like(c), jnp.zeros_like(c), jnp.zeros_like(c))
    return tuple(jnp.stack([t, i]) for t, i in zip((c, s1, s2), ident))


def kernel(x, p, w_in, ssm_conv_w, ssm_conv_b, ssm_dt_bias, ssm_a_log, ssm_d, ssm_norm_w, sc_conv_w,
           w_out, ln1_g, ln1_b, w_up, w_down, ln2_g, ln2_b, w_pe, w_gate, ln3_g, ln3_b):
    bsz, s_len, d_model = x.shape
    depth = w_in.shape[0]
    t = bsz * s_len
    d_ple = p.shape[-1]
    alpha = (2.0 * depth) ** 0.25
    tm = _tile(t, MM_TM)
    assert s_len % tm == 0
    rope_blocks = s_len // tm
    rope = _rope_tables(s_len)
    rope_specs = [pl.BlockSpec((None, tm, HEAD_DIM), lambda j, i: (j // 2, i % rope_blocks, 0))] * 3
    n_grp = len(ATTN_DILATIONS)
    w_in_t = jnp.swapaxes(w_in, 1, 2)

    def flat(a):
        return a.reshape(t, a.shape[-1])

    def ln_resid(y, stats, gain, bias):
        return (flat(y), flat(stats[0]), flat(stats[1]), gain, bias)

    resid = (flat(x),)
    xb, = _layer_norm(x, ln1_g[0], ln1_b[0], normalize=False, emit_f32=False)
    for i in range(depth):
        xbf = flat(xb)
        outs = []
        for g, dil in enumerate(ATTN_DILATIONS):
            qkv = _mmw([xbf], [(w_in_t, d_model, 0, lambda j, g=g: j * n_grp + g)], i,
                       3 * D_AGRP, BF16, mode="rope", extra_inputs=rope, extra_specs=rope_specs,
                       w_t=True, stream=(bsz, dil))
            outs.append(_attn_group(qkv.reshape(bsz, dil, s_len // dil, 3 * D_AGRP)))
        y_attn = _attn_mix([o for o, _ in outs], [l for _, l in outs])

        zx = _mmw([xbf], [(w_in_t, d_model, 0, lambda j: U_Z // MM_TN + j)], i, D_ZX, F32, w_t=True)
        dt = _mmw([xbf], [(w_in_t, d_model, 0, lambda j: U_DT // LANES)], i, LANES, F32, tn=LANES, w_t=True)
        sc = _mmw([xbf], [(w_in_t, d_model, 0, lambda j: U_DT // MM_TN + j)], i, 3 * D_SC, F32,
                  w_shift=U_SC - U_DT, w_t=True)
        y_ssm = _ssd(zx.reshape(bsz, s_len, D_ZX), dt.reshape(bsz, s_len, LANES), ssm_conv_w[i], ssm_conv_b[i],
                     ssm_dt_bias[i], ssm_a_log[i], ssm_d[i], ssm_norm_w[i])
        y_sc = _short_conv(sc.reshape(bsz, s_len, 3 * D_SC), sc_conv_w[i])

        y1 = _mmw([flat(y_attn), flat(y_ssm), flat(y_sc)],
                  [(w_out, D_ATTN, 0, lambda j: j), (w_out, D_SSM, 1, lambda j: j),
                   (w_out, D_SC, (D_ATTN + D_SSM) // D_SC, lambda j: j)], i,
                  d_model, F32, mode="resid", resid=resid, alpha=alpha).reshape(bsz, s_len, d_model)
        xb, *stats = _layer_norm(y1, ln1_g[i], ln1_b[i], emit_f32=False)
        resid = ln_resid(y1, stats, ln1_g[i], ln1_b[i])

        hid = _mmw([flat(xb)], [(w_up, d_model, 0, lambda j: j)], i, w_up.shape[2], BF16, mode="relu2")
        y2 = _mmk_resid(hid, w_down, i, resid, alpha).reshape(bsz, s_len, d_model)
        xb, *stats = _layer_norm(y2, ln2_g[i], ln2_b[i], emit_f32=False)
        resid = ln_resid(y2, stats, ln2_g[i], ln2_b[i])

        y3 = _mmw([flat(xb)], [(w_gate, d_model, 0, lambda j: j)], i, d_model, F32, mode="gate",
                  resid=resid, extra_inputs=(flat(p[i]), w_pe[i]),
                  extra_specs=(pl.BlockSpec((tm, d_ple), lambda j, i_: (i_, 0)),
                               pl.BlockSpec((d_ple, MM_TN), lambda j, i_: (0, j))),
                  alpha=alpha).reshape(bsz, s_len, d_model)
        if i == depth - 1:
            out, = _layer_norm(y3, ln3_g[i], ln3_b[i], emit_bf16=False)
            return out
        xb, *stats = _layer_norm(y3, ln3_g[i], ln3_b[i], emit_f32=False)
        resid = ln_resid(y3, stats, ln3_g[i], ln3_b[i])
```

```python
import functools

import jax
import jax.numpy as jnp
from jax import lax
from jax.experimental import pallas as pl
from jax.experimental.pallas import tpu as pltpu

HEAD_DIM = 128
ATTN_DILATIONS = (1, 4, 16)
ATTN_HPG = 4
ATTN_BLOCK = 128
ATTN_ROWS = 32
D_ATTN = ATTN_HPG * len(ATTN_DILATIONS) * HEAD_DIM
D_AGRP = ATTN_HPG * HEAD_DIM
ROPE_THETA = 500000.0
ROPE_HALF = HEAD_DIM // 8
D_SSM = 1536
SSM_HEAD_DIM = 64
SSM_HEADS = D_SSM // SSM_HEAD_DIM
SSM_GROUPS = 4
SSM_HPG = SSM_HEADS // SSM_GROUPS
SSM_GW = D_SSM // SSM_GROUPS
SSM_STATE = 128
SSM_CONV = 4
SSM_CHUNK = 128
SSM_UNROLL = 2
D_SC = 1024
SC_CONV = 3
LN_EPS = 1e-5
RMS_EPS = 1e-5

U_Z = 3 * D_ATTN
U_DT = U_Z + 2 * D_SSM + 2 * SSM_GROUPS * SSM_STATE
U_SC = U_DT + SSM_HEADS
D_ZX = U_DT - U_Z
ZX_XS = D_SSM
ZX_B = 2 * D_SSM
ZX_C = ZX_B + SSM_GROUPS * SSM_STATE

LANES = 128
SUBLANES = 8
VMEM_LIMIT_BYTES = 60 * 1024 * 1024

BF16 = jnp.bfloat16
F32 = jnp.float32

MM_TM = 1024
MM_TN = 512
LN_TM = 512
LN_ROW_CHUNK = 8
LN_UNROLL = 16
PERM_ROWS = 256


def _params(sem):
    return pltpu.CompilerParams(dimension_semantics=sem, vmem_limit_bytes=VMEM_LIMIT_BYTES)


def _tile(n, pref):
    t = min(n, pref)
    while n % t:
        t -= 1
    return t


def _log2(n):
    assert n > 0 and n & (n - 1) == 0, n
    return n.bit_length() - 1


def _resid_term(r_refs, rs, alpha):
    if len(r_refs) == 1:
        return alpha * r_refs[0][rs, :]
    y_ref, mu_ref, rstd_ref, ag_ref, ab_ref = r_refs
    rep = y_ref.shape[1] // LANES
    mu = jnp.tile(mu_ref[rs, :], (1, rep))
    rstd = jnp.tile(rstd_ref[rs, :], (1, rep))
    return (y_ref[rs, :] - mu) * rstd * ag_ref[...] + ab_ref[...]


def _resid_operands(resid, alpha, tm, tn, index):
    if len(resid) == 1:
        return [resid[0]], [pl.BlockSpec((tm, tn), index)]
    y, mu, rstd, gain, bias = resid
    d = y.shape[1]

    def rows(*g):
        return (index(*g)[0], 0)

    def cols(*g):
        return (0, index(*g)[1])

    return ([y, mu, rstd, (alpha * gain).reshape(1, d), (alpha * bias).reshape(1, d)],
            [pl.BlockSpec((tm, tn), index), pl.BlockSpec((tm, LANES), rows), pl.BlockSpec((tm, LANES), rows),
             pl.BlockSpec((1, tn), cols), pl.BlockSpec((1, tn), cols)])


def _mmw_kernel(*refs, n_x, mode, alpha, w_shift, w_t, n_r, dil):
    xs, ws = refs[:n_x], refs[n_x:2 * n_x]
    pos = 2 * n_x
    if w_shift:
        w_next_ref = refs[pos]
        pos += 1
    if mode == "rope":
        c_ref, s1_ref, s2_ref = refs[pos:pos + 3]
        pos += 3
    elif mode == "resid":
        r_refs = refs[pos:pos + n_r]
        pos += n_r
    elif mode == "gate":
        r_refs = refs[pos:pos + n_r]
        p_ref, wpe_ref = refs[pos + n_r:pos + n_r + 2]
        pos += n_r + 2
    o_ref = refs[pos]
    wbs = refs[pos + 1:pos + 1 + n_x]
    stage_ref = refs[pos + 1 + n_x] if dil > 1 else None

    @pl.when(pl.program_id(1) == 0)
    def _():
        if w_shift:
            keep = wbs[0].shape[0] - w_shift
            wbs[0][:keep, :] = ws[0][w_shift:, :].astype(BF16)
            wbs[0][keep:, :] = w_next_ref[:w_shift, :].astype(BF16)
        else:
            for w_ref, wb_ref in zip(ws, wbs):
                wb_ref[...] = w_ref[...].astype(BF16)

    contract = (((1,), (1,)), ((), ())) if w_t else (((1,), (0,)), ((), ()))
    tm, tn = xs[0].shape[0], o_ref.shape[-1]
    halves = {"rope": 4, "gate": 4, "resid": 4}.get(mode, 1)
    halves = halves if tm % (halves * 2 * SUBLANES) == 0 else 1
    rows_per = tm // halves
    if dil > 1:
        rows_per = min(tm, PERM_ROWS)
        halves = tm // rows_per
        perm = jnp.where(_stream_perm(rows_per, dil, inverse=False), 1.0, 0.0).astype(BF16)
    if mode in ("resid", "gate"):
        for hf in range(halves):
            rs = slice(hf * rows_per, (hf + 1) * rows_per)
            o_ref[rs, :] = _resid_term(r_refs, rs, alpha)

    for hf in range(halves):
        rs = slice(hf * rows_per, (hf + 1) * rows_per)
        acc = None
        for x_ref, wb_ref in zip(xs, wbs):
            part = lax.dot_general(x_ref[rs, :], wb_ref[...], contract, preferred_element_type=F32)
            acc = part if acc is None else acc + part

        if mode == "rope":
            rep = tn // HEAD_DIM
            acc = (acc * jnp.tile(c_ref[rs, :], (1, rep))
                   + pltpu.roll(acc, tn - ROPE_HALF, 1) * jnp.tile(s1_ref[rs, :], (1, rep))
                   + pltpu.roll(acc, ROPE_HALF, 1) * jnp.tile(s2_ref[rs, :], (1, rep)))
        elif mode == "relu2":
            acc = jnp.square(jnp.maximum(acc, 0.0))
        elif mode == "resid":
            acc = o_ref[rs, :] + acc
        elif mode == "gate":
            emb = jnp.dot(p_ref[rs, :].astype(BF16), wpe_ref[...].astype(BF16), preferred_element_type=F32)
            acc = o_ref[rs, :] + jax.nn.sigmoid(acc) * emb
        (stage_ref if dil > 1 else o_ref)[rs, :] = acc.astype(o_ref.dtype)

    if dil > 1:
        n = rows_per // dil
        for hf in range(halves):
            rs = slice(hf * rows_per, (hf + 1) * rows_per)
            moved = jnp.dot(perm, stage_ref[rs, :], preferred_element_type=F32).astype(o_ref.dtype)
            for r in range(dil):
                o_ref[r, hf * n:(hf + 1) * n, :] = moved[r * n:(r + 1) * n]


def _mmw(xs, w_specs, layer, n, out_dtype, mode="plain", extra_inputs=(), extra_specs=(), alpha=1.0, tn=MM_TN,
         w_shift=0, w_t=False, resid=None, stream=None, tm_pref=MM_TM):
    t = xs[0].shape[0]
    tm = _tile(t, tm_pref)
    assert n % tn == 0
    dil = 1
    out_spec = pl.BlockSpec((tm, tn), lambda j, i: (i, j))
    out_shape = jax.ShapeDtypeStruct((t, n), out_dtype)
    if stream is not None and stream[1] > 1:
        batch, dil = stream
        per_seq = t // batch // tm
        assert tm % PERM_ROWS == 0 or tm < PERM_ROWS
        out_spec = pl.BlockSpec((None, dil, tm // dil, tn), lambda j, i: (i // per_seq, 0, i % per_seq, j))
        out_shape = jax.ShapeDtypeStruct((batch, dil, t // batch // dil, n), out_dtype)
    in_specs = [pl.BlockSpec((tm, x.shape[1]), lambda j, i: (i, 0)) for x in xs]
    w_arrays = [w for w, _, _, _ in w_specs]

    def w_spec(kdim, kblk, cfn, off=0, **kw):
        if w_t:
            return pl.BlockSpec((None, tn, kdim), lambda j, i: (layer, cfn(j) + off, kblk), **kw)
        return pl.BlockSpec((None, kdim, tn), lambda j, i: (layer, kblk, cfn(j) + off), **kw)

    for _, kdim, kblk, cfn in w_specs:
        in_specs.append(w_spec(kdim, kblk, cfn))
    if w_shift:
        assert w_t and len(w_specs) == 1 and w_shift % SUBLANES == 0
        w, kdim, kblk, cfn = w_specs[0]
        in_specs.append(w_spec(kdim, kblk, cfn, off=1, pipeline_mode=pl.Buffered(1)))
        w_arrays.append(w)
    n_r = 0
    if resid is not None:
        r_arrays, r_specs = _resid_operands(resid, alpha, tm, tn, lambda j, i: (i, j))
        n_r = len(r_arrays)
        extra_inputs = tuple(r_arrays) + tuple(extra_inputs)
        extra_specs = tuple(r_specs) + tuple(extra_specs)
    return pl.pallas_call(
        functools.partial(_mmw_kernel, n_x=len(xs), mode=mode, alpha=alpha, w_shift=w_shift, w_t=w_t, n_r=n_r,
                          dil=dil),
        grid=(n // tn, t // tm),
        in_specs=in_specs + list(extra_specs),
        out_specs=out_spec,
        out_shape=out_shape,
        scratch_shapes=([pltpu.VMEM((tn, kdim) if w_t else (kdim, tn), BF16) for _, kdim, _, _ in w_specs]
                        + ([pltpu.VMEM((tm, tn), out_dtype)] if dil > 1 else [])),
        compiler_params=_params(("arbitrary", "arbitrary")),
    )(*xs, *w_arrays, *extra_inputs)


def _mmk_kernel(x_ref, w_ref, *refs, alpha, row_chunk):
    r_refs, o_ref, wb_ref = refs[:-2], refs[-2], refs[-1]
    k = pl.program_id(2)

    chunks = [slice(c * row_chunk, (c + 1) * row_chunk) for c in range(o_ref.shape[0] // row_chunk)]

    @pl.when(k == 0)
    def _():
        for rs in chunks:
            o_ref[rs, :] = _resid_term(r_refs, rs, alpha)

    wb_ref[...] = w_ref[...].astype(BF16)
    for rs in chunks:
        o_ref[rs, :] += jnp.dot(x_ref[rs, :], wb_ref[...], preferred_element_type=F32)


def _mmk_resid(x, w, layer, resid, alpha):
    t, kdim = x.shape
    n = w.shape[2]
    tm, tn, tk = _tile(t, 2048), _tile(n, 1024), _tile(kdim, 1024)
    r_arrays, r_specs = _resid_operands(resid, alpha, tm, tn, lambda i, j, k: (i, j))
    return pl.pallas_call(
        functools.partial(_mmk_kernel, alpha=alpha, row_chunk=_tile(tm, 512)),
        grid=(t // tm, n // tn, kdim // tk),
        in_specs=[pl.BlockSpec((tm, tk), lambda i, j, k: (i, k)),
                  pl.BlockSpec((None, tk, tn), lambda i, j, k: (layer, k, j)),
                  *r_specs],
        out_specs=pl.BlockSpec((tm, tn), lambda i, j, k: (i, j)),
        out_shape=jax.ShapeDtypeStruct((t, n), F32),
        scratch_shapes=[pltpu.VMEM((tk, tn), BF16)],
        compiler_params=_params(("parallel", "parallel", "arbitrary")),
    )(x, w, *r_arrays)


def _stream_perm(tm, dil, inverse):
    n = tm // dil
    a = lax.broadcasted_iota(jnp.int32, (tm, tm), 0)
    b = lax.broadcasted_iota(jnp.int32, (tm, tm), 1)
    if inverse:
        src = jnp.bitwise_and(a, dil - 1) * n + lax.shift_right_logical(a, _log2(dil))
    else:
        src = jnp.bitwise_and(a, n - 1) * dil + lax.shift_right_logical(a, _log2(n))
    return b == src


def _ln_kernel(*refs, normalize, emit_f32, emit_bf16):
    y_ref, g_ref, b_ref = refs[:3]
    outs = list(refs[3:])
    f32_ref = outs.pop(0) if emit_f32 else None
    bf16_ref = outs.pop(0) if emit_bf16 else None
    emit_stats = normalize and not emit_f32
    if emit_stats:
        mu_ref, rstd_ref = outs.pop(0), outs.pop(0)
    tm = y_ref.shape[0]
    rows = _tile(tm, LN_ROW_CHUNK)

    def norm_rows(ri, carry):
        rs = pl.ds(pl.multiple_of(ri * rows, rows), rows)
        y = y_ref[rs, :]
        if normalize:
            mu = jnp.mean(y, axis=-1, keepdims=True)
            yc = y - mu
            var = jnp.mean(jnp.square(yc), axis=-1, keepdims=True)
            rstd = lax.rsqrt(var + LN_EPS)
            y = yc * rstd * g_ref[...] + b_ref[...]
            if emit_stats:
                mu_ref[rs, :] = jnp.broadcast_to(mu, (rows, LANES))
                rstd_ref[rs, :] = jnp.broadcast_to(rstd, (rows, LANES))
        if emit_f32:
            f32_ref[rs, :] = y
        if emit_bf16:
            bf16_ref[rs, :] = y.astype(BF16)
        return carry

    lax.fori_loop(0, tm // rows, norm_rows, 0, unroll=min(LN_UNROLL, tm // rows))


def _layer_norm(y3, gain, bias, normalize=True, emit_f32=True, emit_bf16=True):
    b, s, d = y3.shape
    tm = _tile(s, LN_TM)
    tok = pl.BlockSpec((None, tm, d), lambda bi, i: (bi, i, 0))
    vec = pl.BlockSpec((1, d), lambda bi, i: (0, 0))
    out_specs, out_shape = [], []
    if emit_f32:
        out_specs.append(tok)
        out_shape.append(jax.ShapeDtypeStruct((b, s, d), F32))
    if emit_bf16:
        out_specs.append(tok)
        out_shape.append(jax.ShapeDtypeStruct((b, s, d), BF16))
    if normalize and not emit_f32:
        for _ in range(2):
            out_specs.append(pl.BlockSpec((None, tm, LANES), lambda bi, i: (bi, i, 0)))
            out_shape.append(jax.ShapeDtypeStruct((b, s, LANES), F32))
    return pl.pallas_call(
        functools.partial(_ln_kernel, normalize=normalize, emit_f32=emit_f32, emit_bf16=emit_bf16),
        grid=(b, s // tm),
        in_specs=[tok, vec, vec],
        out_specs=out_specs,
        out_shape=out_shape,
        compiler_params=_params(("parallel", "parallel")),
    )(y3, gain.reshape(1, d), bias.reshape(1, d))


def _attn_kernel(q_ref, kc_ref, kp_ref, vc_ref, vp_ref, o_ref, l_ref, s_scr, p_scr, m_scr, bias_scr, *, tq, scale):
    n = pl.program_id(2)
    blk = ATTN_BLOCK
    units = [(h, i) for h in range(ATTN_HPG) for i in range(tq // blk)]
    nt = (((1,), (1,)), ((), ()))

    def keys_vals(ref_c, ref_p, h, i):
        hs = slice(h * HEAD_DIM, (h + 1) * HEAD_DIM)
        if i == 0:
            return jnp.concatenate([ref_p[:, hs], ref_c[:blk, hs]], axis=0)
        return ref_c[(i - 1) * blk:(i + 1) * blk, hs]

    row = lax.broadcasted_iota(jnp.int32, (blk, 2 * blk), 0)
    col = lax.broadcasted_iota(jnp.int32, (blk, 2 * blk), 1)
    bias_scr[...] = jnp.where(col < blk,
                              jnp.where(col >= row, 0.0, -jnp.inf),
                              jnp.where(col - blk <= row, 0.0, -jnp.inf))
    col_r = lax.broadcasted_iota(jnp.int32, (ATTN_ROWS, 2 * blk), 1)
    first_pen = jnp.where(col_r < blk, jnp.where(n == 0, -jnp.inf, 0.0), 0.0)

    for u, (h, i) in enumerate(units):
        q = q_ref[i * blk:(i + 1) * blk, h * HEAD_DIM:(h + 1) * HEAD_DIM]
        s_scr[u] = lax.dot_general(q, keys_vals(kc_ref, kp_ref, h, i), nt, preferred_element_type=F32)

    for u, (h, i) in enumerate(units):
        for r in range(blk // ATTN_ROWS):
            rs = slice(r * ATTN_ROWS, (r + 1) * ATTN_ROWS)
            s = s_scr[u, rs, :] * scale + bias_scr[rs, :]
            if i == 0:
                s = s + first_pen
            m = jnp.max(s, axis=-1, keepdims=True)
            p_scr[u, rs, :] = jnp.exp(s - m).astype(BF16)
            m_scr[u, rs, :] = jnp.broadcast_to(m, (ATTN_ROWS, HEAD_DIM))

    ones = jnp.ones((2 * blk, HEAD_DIM), BF16)
    for u, (h, i) in enumerate(units):
        v_aug = jnp.concatenate([keys_vals(vc_ref, vp_ref, h, i), ones], axis=1)
        pv = jnp.dot(p_scr[u], v_aug, preferred_element_type=F32)
        den = pv[:, HEAD_DIM:]
        rs, hs = slice(i * blk, (i + 1) * blk), slice(h * HEAD_DIM, (h + 1) * HEAD_DIM)
        o_ref[rs, hs] = pv[:, :HEAD_DIM] / den
        l_ref[rs, hs] = m_scr[u] + jnp.log(den)


def _attn_group(qkv):
    b, dil, sub, _ = qkv.shape
    tq = _tile(sub, 512)
    per = tq // ATTN_BLOCK
    n_units = ATTN_HPG * per

    def cur(c):
        return pl.BlockSpec((None, None, tq, D_AGRP), lambda bi, r, n: (bi, r, n, c))

    def prev(c):
        return pl.BlockSpec((None, None, ATTN_BLOCK, D_AGRP),
                            lambda bi, r, n: (bi, r, jnp.maximum(n * per - 1, 0), c))

    out_sds = jax.ShapeDtypeStruct((b, dil, sub, D_AGRP), F32)
    return pl.pallas_call(
        functools.partial(_attn_kernel, tq=tq, scale=HEAD_DIM ** -0.5),
        grid=(b, dil, sub // tq),
        in_specs=[cur(0), cur(1), prev(1), cur(2), prev(2)],
        out_specs=[cur(0), cur(0)],
        out_shape=[out_sds, out_sds],
        scratch_shapes=[pltpu.VMEM((n_units, ATTN_BLOCK, 2 * ATTN_BLOCK), F32),
                        pltpu.VMEM((n_units, ATTN_BLOCK, 2 * ATTN_BLOCK), BF16),
                        pltpu.VMEM((n_units, ATTN_BLOCK, HEAD_DIM), F32),
                        pltpu.VMEM((ATTN_BLOCK, 2 * ATTN_BLOCK), F32)],
        compiler_params=_params(("parallel", "parallel", "arbitrary")),
    )(qkv, qkv, qkv, qkv, qkv)


def _to_token_order(x_ref, dil):
    if dil == 1:
        return x_ref[0]
    n, w = x_ref.shape[1], x_ref.shape[2]
    x = x_ref[...].reshape(dil * n, w)
    perm = jnp.where(_stream_perm(dil * n, dil, inverse=True), 1.0, 0.0).astype(BF16)
    hi = x.astype(BF16)
    r1 = x - hi.astype(F32)
    mid = r1.astype(BF16)
    lo = (r1 - mid.astype(F32)).astype(BF16)
    return (jnp.dot(perm, hi, preferred_element_type=F32)
            + jnp.dot(perm, mid, preferred_element_type=F32)
            + jnp.dot(perm, lo, preferred_element_type=F32))


def _attn_mix_kernel(o0, o1, o2, l0, l1, l2, y_ref):
    os_ = [_to_token_order(o, d) for o, d in zip((o0, o1, o2), ATTN_DILATIONS)]
    ls = [_to_token_order(l, d) for l, d in zip((l0, l1, l2), ATTN_DILATIONS)]
    m = jnp.maximum(jnp.maximum(ls[0], ls[1]), ls[2])
    es = [jnp.exp(l - m) for l in ls]
    inv = 1.0 / (es[0] + es[1] + es[2])
    for g in range(len(ATTN_DILATIONS)):
        y_ref[:, g * D_AGRP:(g + 1) * D_AGRP] = (os_[g] * (es[g] * inv)).astype(y_ref.dtype)


def _attn_mix(os_, ls_):
    b, _, s, _ = os_[0].shape
    tm = _tile(s, PERM_ROWS)
    specs = [pl.BlockSpec((None, d, tm // d, D_AGRP), lambda bi, i: (bi, 0, i, 0)) for d in ATTN_DILATIONS]
    return pl.pallas_call(
        _attn_mix_kernel,
        grid=(b, s // tm),
        in_specs=specs + specs,
        out_specs=pl.BlockSpec((None, tm, D_ATTN), lambda bi, i: (bi, i, 0)),
        out_shape=jax.ShapeDtypeStruct((b, s, D_ATTN), BF16),
        compiler_params=_params(("parallel", "parallel")),
    )(*os_, *ls_)


def _shift_rows(cur, halo, sh):
    rolled = pltpu.roll(cur, sh, 0)
    hr = pltpu.roll(halo, sh, 0)
    row = lax.broadcasted_iota(jnp.int32, halo.shape, 0)
    first = jnp.where(row < sh, hr, rolled[:SUBLANES])
    return jnp.concatenate([first, rolled[SUBLANES:]], axis=0)


def _causal_conv(cur, halo, w):
    kk = w.shape[0]
    acc = w[kk - 1:kk] * cur
    for sh in range(1, kk):
        acc = acc + w[kk - 1 - sh:kk - sh] * _shift_rows(cur, halo, sh)
    return acc


def _sc_kernel(b_ref, c_ref, h_ref, ch_ref, hh_ref, w_ref, o_ref):
    s = pl.program_id(1)
    g = c_ref[...] * h_ref[...]
    gh = ch_ref[...] * hh_ref[...] * jnp.where(s > 0, 1.0, 0.0)
    o_ref[...] = (b_ref[...] * _causal_conv(g, gh, w_ref[...])).astype(o_ref.dtype)


def _short_conv(sc, conv_w):
    b, s, _ = sc.shape
    ts = _tile(s, 512)
    per = ts // SUBLANES

    def cur(j):
        return pl.BlockSpec((None, ts, D_SC), lambda bi, si: (bi, si, j))

    def halo(j):
        return pl.BlockSpec((None, SUBLANES, D_SC), lambda bi, si: (bi, jnp.maximum(si * per - 1, 0), j))

    return pl.pallas_call(
        _sc_kernel,
        grid=(b, s // ts),
        in_specs=[cur(0), cur(1), cur(2), halo(1), halo(2),
                  pl.BlockSpec((SC_CONV, D_SC), lambda bi, si: (0, 0))],
        out_specs=pl.BlockSpec((None, ts, D_SC), lambda bi, si: (bi, si, 0)),
        out_shape=jax.ShapeDtypeStruct((b, s, D_SC), BF16),
        compiler_params=_params(("parallel", "arbitrary")),
    )(sc, sc, sc, sc, sc, conv_w)


def _ssd_kernel(xs_ref, z_ref, b_ref, c_ref, dt_ref,
                wx_ref, bx_ref, wb_ref, bb_ref, wc_ref, bc_ref,
                dtb_ref, alog_ref, dsk_ref, nw_ref,
                o_ref,
                h_ref, xp_ref, bp_ref, cp_ref, y_ref, *, tc):
    g = pl.program_id(1)
    s = pl.program_id(2)
    q = SSM_CHUNK
    p = SSM_HEAD_DIM
    pads = ((xs_ref, xp_ref, wx_ref, bx_ref), (b_ref, bp_ref, wb_ref, bb_ref), (c_ref, cp_ref, wc_ref, bc_ref))

    @pl.when(s == 0)
    def _():
        h_ref[...] = jnp.zeros_like(h_ref)
        for _, pad_ref, _, _ in pads:
            pad_ref[:SUBLANES, :] = jnp.zeros((SUBLANES, pad_ref.shape[1]), F32)

    for raw_ref, pad_ref, _, _ in pads:
        pad_ref[SUBLANES:, :] = raw_ref[...]

    dt_shift = (LANES - g * SSM_HPG) % LANES
    a_row = -jnp.exp(alog_ref[...])
    dskip = dsk_ref[...]
    li = lax.broadcasted_iota(jnp.int32, (q, q), 0)
    si = lax.broadcasted_iota(jnp.int32, (q, q), 1)
    causal = li >= si
    tri = jnp.where(causal, 1.0, 0.0).astype(F32)
    first_head = lax.broadcasted_iota(jnp.int32, (q, 2 * p), 1) < p
    nt = (((1,), (1,)), ((), ()))

    def conv_silu(pad_ref, w_ref, bias_ref, r0):
        w = w_ref[...]
        win = pad_ref[pl.ds(r0, q + SUBLANES), :]
        acc = bias_ref[...]
        for k in range(SSM_CONV):
            back = SSM_CONV - 1 - k
            shifted = win if back == 0 else pltpu.roll(win, back, 0)
            acc = acc + w[k:k + 1] * shifted[SUBLANES:]
        return acc * jax.nn.sigmoid(acc)

    def chunk(ci, carry):
        r0 = pl.multiple_of(ci * q, q)
        xq = conv_silu(xp_ref, wx_ref, bx_ref, r0)
        bq = conv_silu(bp_ref, wb_ref, bb_ref, r0)
        cq = conv_silu(cp_ref, wc_ref, bc_ref, r0)
        dt_raw = pltpu.roll(dt_ref[pl.ds(r0, q), :], dt_shift, 1)
        dtq = jax.nn.softplus(dt_raw + dtb_ref[...])
        cs = jnp.dot(tri, dtq * a_row, precision=lax.Precision.HIGHEST,
                     preferred_element_type=F32)
        cs_t = cs.T
        dt_t = dtq.T
        bb = bq.astype(BF16)
        cb16 = cq.astype(BF16)
        cb = lax.dot_general(cb16, bb, nt, preferred_element_type=F32)
        b_t = bq.T
        hprev = h_ref[...]
        y_off = jnp.dot(cb16, hprev.astype(BF16), preferred_element_type=F32)
        for jp in range(SSM_HPG // 2):
            cols = slice(jp * 2 * p, (jp + 1) * 2 * p)
            x2 = xq[:, cols]
            y_d, st, e_col, e_last = None, None, [], []
            for half in range(2):
                j = 2 * jp + half
                xh = (jnp.where(first_head, x2, 0.0) if half == 0 else jnp.where(first_head, 0.0, x2)).astype(BF16)
                col = cs[:, j:j + 1]
                row = cs_t[j:j + 1, :]
                dtrow = dt_t[j:j + 1, :]
                decay = jnp.exp(jnp.where(causal, col - row, -jnp.inf))
                m = (cb * decay * dtrow).astype(BF16)
                part = jnp.dot(m, xh, preferred_element_type=F32)
                y_d = part if y_d is None else y_d + part
                last = cs[q - 1:q, j:j + 1]
                w_row = jnp.exp(last - row) * dtrow
                part = jnp.dot((b_t * w_row).astype(BF16), xh, preferred_element_type=F32)
                st = part if st is None else st + part
                e_col.append(jnp.exp(col))
                e_last.append(jnp.exp(last))
            y_ref[:, cols] = (y_d + y_off[:, cols] * jnp.where(first_head, e_col[0], e_col[1])
                              + dskip[:, cols] * x2)
            h_ref[:, cols] = hprev[:, cols] * jnp.where(first_head[:1], e_last[0], e_last[1]) + st

        z = z_ref[pl.ds(r0, q), :]
        yv = y_ref[...] * (z * jax.nn.sigmoid(z))
        ms = jnp.mean(jnp.square(yv), axis=-1, keepdims=True)
        o_ref[pl.ds(r0, q), :] = (yv * lax.rsqrt(ms + RMS_EPS) * nw_ref[...]).astype(o_ref.dtype)
        return carry

    lax.fori_loop(0, tc // q, chunk, 0, unroll=min(SSM_UNROLL, tc // q))

    for _, pad_ref, _, _ in pads:
        pad_ref[:SUBLANES, :] = pad_ref[tc:, :]


def _ssd(zx, dt, conv_w, conv_b, dt_bias, a_log, d_skip, norm_w):
    b, s, _ = zx.shape
    tc = _tile(s, 2048)
    gw, ns, hpg = SSM_GW, SSM_STATE, SSM_HPG

    def tok(width, base):
        return pl.BlockSpec((None, tc, width), lambda bi, g, si: (bi, si, base // width + g))

    def par(rows, width):
        return pl.BlockSpec((rows, width), lambda bi, g, si: (0, g))

    def grp(arr):
        a = arr.reshape(SSM_GROUPS, 1, hpg).astype(F32)
        return jnp.pad(a, ((0, 0), (0, 0), (0, LANES - hpg)))

    grp_spec = pl.BlockSpec((None, 1, LANES), lambda bi, g, si: (g, 0, 0))
    wx, wb, wc = conv_w[:, :D_SSM], conv_w[:, D_SSM:D_SSM + 4 * ns], conv_w[:, D_SSM + 4 * ns:]
    cb2 = conv_b.reshape(1, -1)
    bx, bb, bc = cb2[:, :D_SSM], cb2[:, D_SSM:D_SSM + 4 * ns], cb2[:, D_SSM + 4 * ns:]
    d_exp = jnp.repeat(d_skip.astype(F32), SSM_HEAD_DIM).reshape(1, D_SSM)
    return pl.pallas_call(
        functools.partial(_ssd_kernel, tc=tc),
        grid=(b, SSM_GROUPS, s // tc),
        in_specs=[
            tok(gw, ZX_XS), tok(gw, 0), tok(ns, ZX_B), tok(ns, ZX_C),
            pl.BlockSpec((None, tc, LANES), lambda bi, g, si: (bi, si, 0)),
            par(SSM_CONV, gw), par(1, gw),
            par(SSM_CONV, ns), par(1, ns),
            par(SSM_CONV, ns), par(1, ns),
            grp_spec, grp_spec,
            par(1, gw), par(1, gw),
        ],
        out_specs=pl.BlockSpec((None, tc, gw), lambda bi, g, si: (bi, si, g)),
        out_shape=jax.ShapeDtypeStruct((b, s, D_SSM), BF16),
        scratch_shapes=[
            pltpu.VMEM((ns, gw), F32),
            pltpu.VMEM((SUBLANES + tc, gw), F32),
            pltpu.VMEM((SUBLANES + tc, ns), F32),
            pltpu.VMEM((SUBLANES + tc, ns), F32),
            pltpu.VMEM((SSM_CHUNK, gw), F32),
        ],
        compiler_params=_params(("parallel", "parallel", "arbitrary")),
    )(zx, zx, zx, zx, dt,
      wx, bx, wb, bb, wc, bc,
      grp(dt_bias), grp(a_log), d_exp, norm_w.reshape(1, D_SSM).astype(F32))


def _rope_tables(s_len):
    pos = jnp.arange(s_len, dtype=jnp.int32)
    inv_freq = ROPE_THETA ** (-jnp.arange(ROPE_HALF, dtype=F32) / ROPE_HALF)
    ang = pos.astype(F32)[:, None] * inv_freq[None, :]
    cos, sin = jnp.cos(ang), jnp.sin(ang)
    rest = jnp.zeros((s_len, HEAD_DIM - 2 * ROPE_HALF), F32)
    z16 = jnp.zeros((s_len, ROPE_HALF), F32)
    c = jnp.concatenate([cos, cos, jnp.ones_like(rest)], axis=1)
    s1 = jnp.concatenate([-sin, z16, rest], axis=1)
    s2 = jnp.concatenate([z16, sin, rest], axis=1)
    ident = (jnp.ones_like(c), jnp.zeros_like(c), jnp.zeros_like(c))
    return tuple(jnp.stack([t, i]) for t, i in zip((c, s1, s2), ident))


def kernel(x, p, w_in, ssm_conv_w, ssm_conv_b, ssm_dt_bias, ssm_a_log, ssm_d, ssm_norm_w, sc_conv_w,
           w_out, ln1_g, ln1_b, w_up, w_down, ln2_g, ln2_b, w_pe, w_gate, ln3_g, ln3_b):
    bsz, s_len, d_model = x.shape
    depth = w_in.shape[0]
    t = bsz * s_len
    d_ple = p.shape[-1]
    alpha = (2.0 * depth) ** 0.25
    tm = _tile(t, MM_TM)
    assert s_len % tm == 0
    rope_blocks = s_len // tm
    rope = _rope_tables(s_len)
    rope_specs = [pl.BlockSpec((None, tm, HEAD_DIM), lambda j, i: (j // 2, i % rope_blocks, 0))] * 3
    n_grp = len(ATTN_DILATIONS)
    w_in_t = jnp.swapaxes(w_in, 1, 2)

    def flat(a):
        return a.reshape(t, a.shape[-1])

    def ln_resid(y, stats, gain, bias):
        return (flat(y), flat(stats[0]), flat(stats[1]), gain, bias)

    resid = (flat(x),)
    xb, = _layer_norm(x, ln1_g[0], ln1_b[0], normalize=False, emit_f32=False)
    for i in range(depth):
        xbf = flat(xb)
        outs = []
        for g, dil in enumerate(ATTN_DILATIONS):
            qkv = _mmw([xbf], [(w_in_t, d_model, 0, lambda j, g=g: j * n_grp + g)], i,
                       3 * D_AGRP, BF16, mode="rope", extra_inputs=rope, extra_specs=rope_specs,
                       w_t=True, stream=(bsz, dil))
            outs.append(_attn_group(qkv.reshape(bsz, dil, s_len // dil, 3 * D_AGRP)))
        y_attn = _attn_mix([o for o, _ in outs], [l for _, l in outs])

        zx = _mmw([xbf], [(w_in_t, d_model, 0, lambda j: U_Z // MM_TN + j)], i, D_ZX, F32, w_t=True)
        dt = _mmw([xbf], [(w_in_t, d_model, 0, lambda j: U_DT // LANES)], i, LANES, F32, tn=LANES, w_t=True)
        sc = _mmw([xbf], [(w_in_t, d_model, 0, lambda j: U_DT // MM_TN + j)], i, 3 * D_SC, F32,
                  w_shift=U_SC - U_DT, w_t=True)
        y_ssm = _ssd(zx.reshape(bsz, s_len, D_ZX), dt.reshape(bsz, s_len, LANES), ssm_conv_w[i], ssm_conv_b[i],
                     ssm_dt_bias[i], ssm_a_log[i], ssm_d[i], ssm_norm_w[i])
        y_sc = _short_conv(sc.reshape(bsz, s_len, 3 * D_SC), sc_conv_w[i])

        y1 = _mmw([flat(y_attn), flat(y_ssm), flat(y_sc)],
                  [(w_out, D_ATTN, 0, lambda j: j), (w_out, D_SSM, 1, lambda j: j),
                   (w_out, D_SC, (D_ATTN + D_SSM) // D_SC, lambda j: j)], i,
                  d_model, F32, mode="resid", resid=resid, alpha=alpha).reshape(bsz, s_len, d_model)
        xb, *stats = _layer_norm(y1, ln1_g[i], ln1_b[i], emit_f32=False)
        resid = ln_resid(y1, stats, ln1_g[i], ln1_b[i])

        hid = _mmw([flat(xb)], [(w_up, d_model, 0, lambda j: j)], i, w_up.shape[2], BF16, mode="relu2",
                   tm_pref=2 * MM_TM)
        y2 = _mmk_resid(hid, w_down, i, resid, alpha).reshape(bsz, s_len, d_model)
        xb, *stats = _layer_norm(y2, ln2_g[i], ln2_b[i], emit_f32=False)
        resid = ln_resid(y2, stats, ln2_g[i], ln2_b[i])

        y3 = _mmw([flat(xb)], [(w_gate, d_model, 0, lambda j: j)], i, d_model, F32, mode="gate",
                  resid=resid, extra_inputs=(flat(p[i]), w_pe[i]),
                  extra_specs=(pl.BlockSpec((tm, d_ple), lambda j, i_: (i_, 0)),
                               pl.BlockSpec((d_ple, MM_TN), lambda j, i_: (0, j))),
                  alpha=alpha).reshape(bsz, s_len, d_model)
        if i == depth - 1:
            out, = _layer_norm(y3, ln3_g[i], ln3_b[i], emit_bf16=False)
            return out
        xb, *stats = _layer_norm(y3, ln3_g[i], ln3_b[i], emit_f32=False)
        resid = ln_resid(y3, stats, ln3_g[i], ln3_b[i])
```
